```python
import math
import jax
import jax.numpy as jnp
from jax import lax
import numpy as np

D_MODEL = 2048
BATCH = 4
SEQ = 2048
DEPTH = 2
DEC_BATCH = 128
DEC_SEQ = 4
PAST_LEN = 16384
PAGE_SIZE = 128

N_GROUPS = 4
GROUP_W = D_MODEL // N_GROUPS
D_FF = 4 * D_MODEL
ROPE_THETA = 10000.0
NORM_EPS = 1e-6
Q_BLOCK = 128
NEG_INF = -1e30
F_MIN = 1e-6
F32 = jnp.float32

DA = 64
DVA = 2 * DA
H_A = GROUP_W // DVA
KVH_A = H_A // 2
DK_B = 128
DV_B = 128
H_B = GROUP_W // DV_B
HB_K = H_B * DK_B
HGRN_CHUNK = 64
DN_C = 128
DR_C = 32
DV_C = 128
H_C = GROUP_W // DV_C
QR_C = 384
KR_C = 128
N_D = 64
H_D = GROUP_W // N_D
W_LORA = 32
A_LORA = 32
G_LORA = 96
RWKV_LN_EPS = 64e-5

A_Q = 2 * H_A * DA
A_K = 2 * KVH_A * DA
A_V = KVH_A * DVA
A_COLS = A_Q + A_K + A_V
B_COLS = 2 * HB_K + 2 * GROUP_W
C_COLS = QR_C + KR_C + DR_C
D_COLS = 3 * GROUP_W + W_LORA + A_LORA + G_LORA
C_IN = A_COLS + B_COLS + C_COLS + D_COLS

kernel_name = 'hybrid_diff_hgrn2_mla_rwkv7_step'


def rmsnorm(x, g, eps=NORM_EPS):
    xf = x.astype(F32)
    y = xf * lax.rsqrt(jnp.mean(xf * xf, axis=-1, keepdims=True) + eps)
    return (y * g.astype(F32)).astype(x.dtype)


def rope(x, pos):
    d = x.shape[-1]
    inv_freq = ROPE_THETA ** (-jnp.arange(0, d, 2, dtype=F32) / d)
    ang = pos.astype(F32)[:, None] * inv_freq[None, :]
    cos = jnp.cos(ang)[None, :, None, :]
    sin = jnp.sin(ang)[None, :, None, :]
    xf = x.astype(F32)
    x1, x2 = xf[..., : d // 2], xf[..., d // 2:]
    return jnp.concatenate([x1 * cos - x2 * sin, x2 * cos + x1 * sin], axis=-1).astype(x.dtype)


def attend_full(q, k, v):
    m, b, h, s, d = q.shape
    qb = min(Q_BLOCK, s)
    nb = s // qb
    q_blocks = jnp.moveaxis(q.astype(F32).reshape(m, b, h, nb, qb, d), 3, 0)
    k_t = jnp.swapaxes(k.astype(F32), -1, -2)
    v_f = v.astype(F32)
    kpos = jnp.arange(s)

    def one_block(args):
        q_i, start = args
        scores = jnp.matmul(q_i, k_t)
        qpos = start + jnp.arange(qb)
        scores = jnp.where(kpos[None, :] <= qpos[:, None], scores, NEG_INF)
        return jnp.matmul(jax.nn.softmax(scores, axis=-1), v_f)

    out = lax.map(one_block, (q_blocks, jnp.arange(nb) * qb))
    return jnp.moveaxis(out, 0, 3).reshape(m, b, h, s, v.shape[-1])


def online_update(carry, s, v):
    m, l, acc = carry
    m_new = jnp.maximum(m, s.max(axis=-1))
    corr = jnp.exp(m - m_new)
    p = jnp.exp(s - m_new[..., None])
    return (m_new, l * corr + p.sum(axis=-1), acc * corr[..., None] + jnp.matmul(p, v.astype(F32)))


def attend_paged(q, k_new, v_new, page_fn, page_table):
    m, b, h, nq, _ = q.shape
    dv = v_new.shape[-1]
    qf = q.astype(F32)
    carry = (jnp.full((m, b, h, nq), NEG_INF, F32), jnp.zeros((m, b, h, nq), F32), jnp.zeros((m, b, h, nq, dv), F32))

    def step(c, phys):
        k, v = page_fn(phys)
        s = jnp.matmul(qf, jnp.swapaxes(k.astype(F32), -1, -2))
        return online_update(c, s, v), None

    carry, _ = lax.scan(step, carry, page_table.T)
    s = jnp.matmul(qf, jnp.swapaxes(k_new.astype(F32), -1, -2))
    s = jnp.where(jnp.tril(jnp.ones((nq, nq), bool)), s, NEG_INF)
    _, l, acc = online_update(carry, s, v_new)
    return acc / l[..., None]


def split_groups(p):
    a1 = A_COLS
    b1 = a1 + B_COLS
    c1 = b1 + C_COLS
    return p[..., :a1], p[..., a1:b1], p[..., b1:c1], p[..., c1:]


def diff_project(pa, pos):
    b, l, _ = pa.shape
    q = rope(pa[..., :A_Q].reshape(b, l, 2 * H_A, DA), pos)
    k = rope(pa[..., A_Q:A_Q + A_K].reshape(b, l, 2 * KVH_A, DA), pos)
    v = pa[..., A_Q + A_K:].reshape(b, l, KVH_A, DVA)
    q_maps = jnp.transpose(q.reshape(b, l, H_A, 2, DA), (3, 0, 2, 1, 4)) * (DA ** -0.5)
    return q_maps, k.reshape(b, l, KVH_A, 2 * DA), v


def diff_kv_maps(k_rows, v_rows):
    b, s = k_rows.shape[:2]
    rep = H_A // KVH_A
    k = jnp.repeat(k_rows.reshape(b, s, KVH_A, 2, DA), rep, axis=2)
    k = jnp.transpose(k, (3, 0, 2, 1, 4))
    v = jnp.transpose(jnp.repeat(v_rows, rep, axis=2), (0, 2, 1, 3))
    return k, v


def diff_lambda(lq1, lk1, lq2, lk2, layer_idx):
    lam_init = 0.8 - 0.6 * math.exp(-0.3 * layer_idx)
    lam = (jnp.exp(jnp.sum(lq1.astype(F32) * lk1.astype(F32)))
           - jnp.exp(jnp.sum(lq2.astype(F32) * lk2.astype(F32))) + lam_init)
    return lam, lam_init


def diff_merge(o, lam, lam_init, subln):
    a = o[0] - lam * o[1]
    a = rmsnorm(a, subln) * (1.0 - lam_init)
    b, h, l, _ = a.shape
    return jnp.transpose(a, (0, 2, 1, 3)).reshape(b, l, h * DVA)


def gla_chunked(q, k, v, log_f, s0):
    b, l, h, dk = q.shape
    dv = v.shape[-1]
    c = min(HGRN_CHUNK, l)
    n = l // c

    def blocks(t):
        return jnp.transpose(t.reshape(b, n, c, h, t.shape[-1]), (1, 0, 3, 2, 4))

    causal = jnp.tril(jnp.ones((c, c), bool))[:, :, None]

    def step(s, inp):
        qc, kc, vc, gc = inp
        cum = jnp.cumsum(gc, axis=2)
        o_inter = jnp.einsum('bhcd,bhde->bhce', qc * jnp.exp(cum), s)
        diff = cum[:, :, :, None, :] - cum[:, :, None, :, :]
        decay = jnp.where(causal, jnp.exp(jnp.where(causal, diff, 0.0)), 0.0)
        att = jnp.einsum('bhid,bhjd,bhijd->bhij', qc, kc, decay)
        o = o_inter + jnp.einsum('bhij,bhje->bhie', att, vc)
        last = cum[:, :, -1:, :]
        s = jnp.exp(last[:, :, 0, :])[..., None] * s + jnp.einsum('bhjd,bhje->bhde', kc * jnp.exp(last - cum), vc)
        return s, o

    s, o = lax.scan(step, s0, (blocks(q), blocks(k), blocks(v), blocks(log_f)))
    return jnp.transpose(o, (1, 0, 3, 2, 4)).reshape(b, l, h, dv), s


def hgrn_mix(pb, lb, s0, g_norm):
    b, l, _ = pb.shape
    pf = pb.astype(F32)
    q = jax.nn.silu(pf[..., :HB_K]).reshape(b, l, H_B, DK_B)
    f_raw = pf[..., HB_K:2 * HB_K]
    i = pf[..., 2 * HB_K:2 * HB_K + GROUP_W].reshape(b, l, H_B, DV_B)
    g = pf[..., 2 * HB_K + GROUP_W:]
    f = lb + (1.0 - lb) * jax.nn.sigmoid(f_raw)
    log_f = jnp.log(jnp.maximum(f, F_MIN)).reshape(b, l, H_B, DK_B)
    k = ((1.0 - lb) * jax.nn.sigmoid(-f_raw)).reshape(b, l, H_B, DK_B)
    o, s = gla_chunked(q, k, i, log_f, s0.astype(F32))
    o = rmsnorm(o, g_norm).reshape(b, l, GROUP_W) * jax.nn.silu(g)
    return o, s


def mla_project(pc, pos, q_norm, w_uq, kv_norm, w_uk):
    cq = rmsnorm(pc[..., :QR_C], q_norm)
    q = jnp.einsum('blr,rhd->blhd', cq, w_uq)
    q_nope = q[..., :DN_C]
    q_rope = rope(q[..., DN_C:], pos)
    q_lat = jnp.einsum('blhn,khn->blhk', q_nope, w_uk)
    qc = jnp.transpose(jnp.concatenate([q_lat, q_rope], axis=-1), (0, 2, 1, 3))[None] * ((DN_C + DR_C) ** -0.5)
    c_kv = rmsnorm(pc[..., QR_C:QR_C + KR_C], kv_norm)
    k_rope = rope(pc[..., QR_C + KR_C:][:, :, None, :], pos)[:, :, 0, :]
    return qc, c_kv, k_rope


def mla_kv(c_kv, k_rope):
    k = jnp.concatenate([c_kv, k_rope], axis=-1)[None, :, None]
    return k, c_kv[:, None]


def mla_merge(o, w_uv):
    y = jnp.einsum('bhlk,khv->blhv', o[0], w_uv)
    b, l = y.shape[:2]
    return y.reshape(b, l, H_C * DV_C)


def wkv7_scan(r, log_w, k, v, a, bb, s0):
    def step(s, inp):
        r_t, w_t, k_t, v_t, a_t, b_t = inp
        sa = jnp.einsum('bhvk,bhk->bhv', s, a_t)
        s = (s * jnp.exp(w_t)[:, :, None, :] + sa[..., None] * b_t[:, :, None, :]
             + v_t[..., None] * k_t[:, :, None, :])
        return s, jnp.einsum('bhvk,bhk->bhv', s, r_t)

    xs = tuple(jnp.moveaxis(t.astype(F32), 1, 0) for t in (r, log_w, k, v, a, bb))
    s, ys = lax.scan(step, s0.astype(F32), xs)
    return jnp.moveaxis(ys, 0, 1), s


def head_layernorm(y, w, b):
    bs, l, h, n = y.shape
    mu = jnp.mean(y, axis=-1, keepdims=True)
    var = jnp.mean(jnp.square(y - mu), axis=-1, keepdims=True)
    yn = ((y - mu) * lax.rsqrt(var + RWKV_LN_EPS)).reshape(bs, l, h * n)
    return yn * w.astype(F32) + b.astype(F32)


def rwkv_mix(pd, shift_prev, s0, lp):
    b, l, _ = pd.shape
    G = GROUP_W
    prev = jnp.concatenate([shift_prev[:, None, :].astype(pd.dtype), pd[:, :-1]], axis=1)
    xm = (pd + (prev - pd) * lp['rwkv_mu']).astype(F32)
    r, k, v = xm[..., :G], xm[..., G:2 * G], xm[..., 2 * G:3 * G]
    o0 = 3 * G
    wd = xm[..., o0:o0 + W_LORA]
    ad = xm[..., o0 + W_LORA:o0 + W_LORA + A_LORA]
    gd = xm[..., o0 + W_LORA + A_LORA:]
    w = -jax.nn.softplus(-(lp['rwkv_w0'] + jnp.tanh(wd) @ lp['rwkv_w2'])) - 0.5
    log_decay = -jnp.exp(w)
    a = jax.nn.sigmoid(lp['rwkv_a0'] + ad @ lp['rwkv_a2'])
    g = jax.nn.sigmoid(gd) @ lp['rwkv_g2']

    def heads(t):
        return t.reshape(b, l, H_D, N_D)

    kk = heads(k * lp['rwkv_kk'])
    kk = kk / jnp.maximum(jnp.sqrt(jnp.sum(kk * kk, axis=-1, keepdims=True)), 1e-12)
    k = k * (1.0 + (a - 1.0) * lp['rwkv_ka'])
    y, s = wkv7_scan(heads(r), heads(log_decay), heads(k), heads(v), -kk, kk * heads(a), s0)
    y = head_layernorm(y, lp['rwkv_ln_w'], lp['rwkv_ln_b'])
    bonus = jnp.sum(heads(r) * heads(k) * lp['rwkv_rk'].astype(F32), axis=-1, keepdims=True) * heads(v)
    y = (y + bonus.reshape(b, l, G)) * g
    return y, s, pd[:, -1]


def trunk_layer(x, pos, layer_idx, lp, attend, s_hgrn, s_rwkv, shift):
    h = rmsnorm(x, lp['g_pre_mix'])
    p = h @ lp['w_in']
    pa, pb, pc, pd = split_groups(p)
    qa, k_rows, v_rows = diff_project(pa, pos)
    ka, va = diff_kv_maps(k_rows, v_rows)
    lam, lam_init = diff_lambda(lp['diff_lq1'], lp['diff_lk1'], lp['diff_lq2'], lp['diff_lk2'], layer_idx)
    ya = diff_merge(attend('diff', qa, ka, va), lam, lam_init, lp['diff_subln'])
    yb, s_hgrn = hgrn_mix(pb, lp['hgrn_lb'], s_hgrn, lp['hgrn_norm'])
    qc, c_kv, k_rope = mla_project(pc, pos, lp['mla_q_norm'], lp['mla_w_uq'], lp['mla_kv_norm'], lp['mla_w_uk'])
    kc, vc = mla_kv(c_kv, k_rope)
    yc = mla_merge(attend('mla', qc, kc, vc), lp['mla_w_uv'])
    yd, s_rwkv, shift_last = rwkv_mix(pd, shift, s_rwkv, lp)
    mix = jnp.concatenate([ya, yb, yc, yd], axis=-1).astype(x.dtype)
    x = x + rmsnorm(mix @ lp['w_out'], lp['g_post_mix'])
    hf = rmsnorm(x, lp['g_pre_ffn'])
    u = jnp.square(jax.nn.relu(hf @ lp['w_up']))
    x = x + rmsnorm(u @ lp['w_down'], lp['g_post_ffn'])
    return x, (k_rows, v_rows, c_kv, k_rope, s_hgrn, s_rwkv, shift_last)


def setup_inputs(seed: int = 0) -> dict:
    key = jax.random.key(seed)
    keys = iter(jax.random.split(key, 64))
    n_pages = PAST_LEN // PAGE_SIZE
    used = DEC_BATCH * n_pages
    pool = used + used // 4

    def normal(shape, scale=None):
        z = jax.random.normal(next(keys), shape, F32)
        return z if scale is None else z * scale

    def gain(shape):
        return 1.0 + 0.02 * jax.random.normal(next(keys), shape, F32)

    def uniform(shape, lo, hi):
        return jax.random.uniform(next(keys), shape, F32, minval=lo, maxval=hi)

    d = {}
    d['x_prompt'] = normal((BATCH, SEQ, D_MODEL))
    d['x_sample'] = normal((DEC_BATCH, DEC_SEQ, D_MODEL))
    d['cache_diff_k'] = normal((DEPTH, pool, PAGE_SIZE, KVH_A, 2 * DA))
    d['cache_diff_v'] = normal((DEPTH, pool, PAGE_SIZE, KVH_A, DVA))
    d['cache_mla_latent'] = normal((DEPTH, pool, PAGE_SIZE, KR_C))
    d['cache_mla_rope'] = normal((DEPTH, pool, PAGE_SIZE, DR_C))
    d['state_hgrn'] = normal((DEPTH, DEC_BATCH, H_B, DK_B, DV_B), 0.5)
    d['state_rwkv'] = normal((DEPTH, DEC_BATCH, H_D, N_D, N_D), 0.1)
    d['state_rwkv_shift'] = normal((DEPTH, DEC_BATCH, D_COLS))
    d['page_table'] = jax.random.permutation(next(keys), pool)[:used].reshape(DEC_BATCH, n_pages).astype(jnp.int32)
    d['hgrn_lb_logits'] = normal((DEPTH, HB_K), 0.5)
    d['g_pre_mix'] = gain((DEPTH, D_MODEL))
    d['w_in'] = normal((DEPTH, D_MODEL, C_IN), D_MODEL ** -0.5)
    d['w_out'] = normal((DEPTH, D_MODEL, D_MODEL), D_MODEL ** -0.5)
    d['g_post_mix'] = gain((DEPTH, D_MODEL))
    d['diff_lq1'] = normal((DEPTH, DA), 0.1)
    d['diff_lk1'] = normal((DEPTH, DA), 0.1)
    d['diff_lq2'] = normal((DEPTH, DA), 0.1)
    d['diff_lk2'] = normal((DEPTH, DA), 0.1)
    d['diff_subln'] = gain((DEPTH, DVA))
    d['hgrn_norm'] = gain((DEPTH, DV_B))
    d['mla_q_norm'] = gain((DEPTH, QR_C))
    d['mla_w_uq'] = normal((DEPTH, QR_C, H_C, DN_C + DR_C), QR_C ** -0.5)
    d['mla_kv_norm'] = gain((DEPTH, KR_C))
    d['mla_w_uk'] = normal((DEPTH, KR_C, H_C, DN_C), KR_C ** -0.5)
    d['mla_w_uv'] = normal((DEPTH, KR_C, H_C, DV_C), KR_C ** -0.5)
    d['rwkv_mu'] = uniform((DEPTH, D_COLS), 0.0, 1.0)
    d['rwkv_w0'] = uniform((DEPTH, GROUP_W), -6.0, -1.0)
    d['rwkv_w2'] = normal((DEPTH, W_LORA, GROUP_W), 0.1)
    d['rwkv_a0'] = normal((DEPTH, GROUP_W), 0.1)
    d['rwkv_a2'] = normal((DEPTH, A_LORA, GROUP_W), 0.1)
    d['rwkv_g2'] = normal((DEPTH, G_LORA, GROUP_W), G_LORA ** -0.5)
    d['rwkv_kk'] = 0.85 + 0.02 * normal((DEPTH, GROUP_W))
    d['rwkv_ka'] = gain((DEPTH, GROUP_W))
    d['rwkv_rk'] = normal((DEPTH, H_D, N_D), 0.1)
    d['rwkv_ln_w'] = gain((DEPTH, GROUP_W))
    d['rwkv_ln_b'] = normal((DEPTH, GROUP_W), 0.02)
    d['g_pre_ffn'] = gain((DEPTH, D_MODEL))
    d['w_up'] = normal((DEPTH, D_MODEL, D_FF), D_MODEL ** -0.5)
    d['w_down'] = normal((DEPTH, D_FF, D_MODEL), D_FF ** -0.5)
    d['g_post_ffn'] = gain((DEPTH, D_MODEL))
    return d


def reference(x_prompt, x_sample, cache_diff_k, cache_diff_v, cache_mla_latent, cache_mla_rope,
              state_hgrn, state_rwkv, state_rwkv_shift, page_table,
              hgrn_lb_logits, g_pre_mix, w_in, w_out, g_post_mix,
              diff_lq1, diff_lk1, diff_lq2, diff_lk2, diff_subln, hgrn_norm,
              mla_q_norm, mla_w_uq, mla_kv_norm, mla_w_uk, mla_w_uv,
              rwkv_mu, rwkv_w0, rwkv_w2, rwkv_a0, rwkv_a2, rwkv_g2, rwkv_kk, rwkv_ka, rwkv_rk,
              rwkv_ln_w, rwkv_ln_b, g_pre_ffn, w_up, w_down, g_post_ffn):
    probs = jax.nn.softmax(hgrn_lb_logits.astype(F32), axis=0)
    lower_bounds = jnp.clip(jnp.cumsum(probs, axis=0) - probs[0:1], 0.0, 1.0)

    bp, sp, _ = x_prompt.shape
    bs, ss, _ = x_sample.shape
    past_len = page_table.shape[1] * PAGE_SIZE
    pos_p = jnp.arange(sp)
    pos_s = past_len + jnp.arange(ss)

    def attend_prompt(name, q, k, v):
        return attend_full(q, k, v)

    xp, xs = x_prompt, x_sample
    rows_p, rows_s = [], []
    for l in range(DEPTH):
        lp = dict(
            g_pre_mix=g_pre_mix[l], w_in=w_in[l], w_out=w_out[l], g_post_mix=g_post_mix[l],
            diff_lq1=diff_lq1[l], diff_lk1=diff_lk1[l], diff_lq2=diff_lq2[l], diff_lk2=diff_lk2[l],
            diff_subln=diff_subln[l], hgrn_lb=lower_bounds[l], hgrn_norm=hgrn_norm[l],
            mla_q_norm=mla_q_norm[l], mla_w_uq=mla_w_uq[l], mla_kv_norm=mla_kv_norm[l],
            mla_w_uk=mla_w_uk[l], mla_w_uv=mla_w_uv[l],
            rwkv_mu=rwkv_mu[l], rwkv_w0=rwkv_w0[l], rwkv_w2=rwkv_w2[l], rwkv_a0=rwkv_a0[l],
            rwkv_a2=rwkv_a2[l], rwkv_g2=rwkv_g2[l], rwkv_kk=rwkv_kk[l], rwkv_ka=rwkv_ka[l],
            rwkv_rk=rwkv_rk[l], rwkv_ln_w=rwkv_ln_w[l], rwkv_ln_b=rwkv_ln_b[l],
            g_pre_ffn=g_pre_ffn[l], w_up=w_up[l], w_down=w_down[l], g_post_ffn=g_post_ffn[l])

        def page_diff(phys, l=l):
            return diff_kv_maps(cache_diff_k[l, phys], cache_diff_v[l, phys])

        def page_mla(phys, l=l):
            return mla_kv(cache_mla_latent[l, phys], cache_mla_rope[l, phys])

        page_fns = {'diff': page_diff, 'mla': page_mla}

        def attend_sample(name, q, k, v, page_fns=page_fns):
            return attend_paged(q, k, v, page_fns[name], page_table)

        xp, rp = trunk_layer(xp, pos_p, l, lp, attend_prompt,
                             jnp.zeros((bp, H_B, DK_B, DV_B), F32),
                             jnp.zeros((bp, H_D, N_D, N_D), F32),
                             jnp.zeros((bp, D_COLS), x_prompt.dtype))
        xs, rs = trunk_layer(xs, pos_s, l, lp, attend_sample,
                             state_hgrn[l], state_rwkv[l], state_rwkv_shift[l])
        rows_p.append(rp)
        rows_s.append(rs)

    diff_k_p = jnp.stack([r[0] for r in rows_p])
    diff_v_p = jnp.stack([r[1] for r in rows_p])
    mla_lat_p = jnp.stack([r[2] for r in rows_p])
    mla_rope_p = jnp.stack([r[3] for r in rows_p])
    hgrn_p = jnp.stack([r[4] for r in rows_p])
    rwkv_p = jnp.stack([r[5] for r in rows_p])
    shift_p = jnp.stack([r[6] for r in rows_p])
    diff_k_s = jnp.stack([r[0] for r in rows_s])
    diff_v_s = jnp.stack([r[1] for r in rows_s])
    mla_lat_s = jnp.stack([r[2] for r in rows_s])
    mla_rope_s = jnp.stack([r[3] for r in rows_s])
    hgrn_s = jnp.stack([r[4] for r in rows_s])
    rwkv_s = jnp.stack([r[5] for r in rows_s])
    shift_s = jnp.stack([r[6] for r in rows_s])
    return (xp, xs, diff_k_p, diff_v_p, mla_lat_p, mla_rope_p, hgrn_p, rwkv_p, shift_p,
            diff_k_s, diff_v_s, mla_lat_s, mla_rope_s, hgrn_s, rwkv_s, shift_s)
```

```python
import functools
import math

import jax
import jax.numpy as jnp
from jax import lax
from jax.experimental import pallas as pl
from jax.experimental.pallas import tpu as pltpu

F32 = jnp.float32
BF16 = jnp.bfloat16

D_MODEL = 2048
PAGE_SIZE = 128
N_GROUPS = 4
GROUP_W = D_MODEL // N_GROUPS
D_FF = 4 * D_MODEL
ROPE_THETA = 10000.0
NORM_EPS = 1e-6
NEG_INF = -1e30
F_MIN = 1e-6

DA = 64
DVA = 2 * DA
H_A = GROUP_W // DVA
KVH_A = H_A // 2
DK_B = 128
DV_B = 128
H_B = GROUP_W // DV_B
HB_K = H_B * DK_B
DN_C = 128
DR_C = 32
DV_C = 128
H_C = GROUP_W // DV_C
QR_C = 384
KR_C = 128
N_D = 64
H_D = GROUP_W // N_D
W_LORA = 32
A_LORA = 32
G_LORA = 96
RWKV_LN_EPS = 64e-5

A_Q = 2 * H_A * DA
A_K = 2 * KVH_A * DA
A_V = KVH_A * DVA
A_COLS = A_Q + A_K + A_V
B_COLS = 2 * HB_K + 2 * GROUP_W
C_COLS = QR_C + KR_C + DR_C
D_COLS = 3 * GROUP_W + W_LORA + A_LORA + G_LORA

LANES = 128
VMEM_LIMIT = 48 * 1024 * 1024
ROW_TILE = 512


def _cparams(sem):
    return pltpu.CompilerParams(dimension_semantics=sem, vmem_limit_bytes=VMEM_LIMIT)


def _row_tile(m, cap=ROW_TILE):
    t = cap
    while m % t:
        t //= 2
    return t


def _rms_kernel(x_ref, g_ref, o_ref):
    x = x_ref[...].astype(F32)
    y = x * lax.rsqrt(jnp.mean(x * x, axis=-1, keepdims=True) + NORM_EPS)
    o_ref[...] = (y * g_ref[...]).astype(o_ref.dtype)


def rms_rows(x, g, out_dtype, tm=ROW_TILE):
    m, d = x.shape
    tm = _row_tile(m, tm)
    return pl.pallas_call(
        _rms_kernel,
        grid=(m // tm,),
        in_specs=[pl.BlockSpec((tm, d), lambda i: (i, 0)),
                  pl.BlockSpec((1, d), lambda i: (0, 0))],
        out_specs=pl.BlockSpec((tm, d), lambda i: (i, 0)),
        out_shape=jax.ShapeDtypeStruct((m, d), out_dtype),
        compiler_params=_cparams(("parallel",)),
    )(x, g.reshape(1, d).astype(F32))


def _mm_kernel(*refs, nk, act, post):
    if post:
        x_ref, w_ref, g_ref, res_ref, o_ref = refs[:5]
        scratch = refs[5:]
    else:
        x_ref, w_ref, o_ref = refs[:3]
        scratch = refs[3:]
    part = jnp.dot(x_ref[...].astype(BF16), w_ref[...].astype(BF16),
                   preferred_element_type=F32)

    def finish(acc):
        if act == "relu2":
            r = jnp.maximum(acc, 0.0)
            acc = r * r
        if post:
            y = acc * lax.rsqrt(jnp.mean(acc * acc, axis=-1, keepdims=True) + NORM_EPS)
            acc = res_ref[...] + y * g_ref[...]
        o_ref[...] = acc.astype(o_ref.dtype)

    if nk == 1:
        finish(part)
    else:
        acc_ref = scratch[0]
        k = pl.program_id(2)

        @pl.when(k == 0)
        def _():
            acc_ref[...] = part

        @pl.when(k > 0)
        def _():
            acc_ref[...] += part

        @pl.when(k == nk - 1)
        def _():
            finish(acc_ref[...])


def matmul(x, w, *, tn, tk=None, tm=ROW_TILE, out_dtype=F32, act=None, post=None):
    m, kdim = x.shape
    n = w.shape[1]
    tm = _row_tile(m, tm)
    tk = tk or kdim
    nk = kdim // tk
    assert m % tm == 0 and n % tn == 0 and kdim % tk == 0
    in_specs = [pl.BlockSpec((tm, tk), lambda i, j, k: (i, k)),
                pl.BlockSpec((tk, tn), lambda i, j, k: (k, j))]
    args = [x, w]
    if post:
        assert tn == n
        gain, res = post
        in_specs += [pl.BlockSpec((1, n), lambda i, j, k: (0, 0)),
                     pl.BlockSpec((tm, n), lambda i, j, k: (i, 0))]
        args += [gain.reshape(1, n).astype(F32), res]
    return pl.pallas_call(
        functools.partial(_mm_kernel, nk=nk, act=act, post=bool(post)),
        grid=(m // tm, n // tn, nk),
        in_specs=in_specs,
        out_specs=pl.BlockSpec((tm, tn), lambda i, j, k: (i, j)),
        out_shape=jax.ShapeDtypeStruct((m, n), out_dtype),
        scratch_shapes=[pltpu.VMEM((tm, tn), F32)] if nk > 1 else [],
        compiler_params=_cparams(("parallel", "parallel", "arbitrary")),
    )(*args)


def _diff_lambda(lqk_ref, lam_init):
    lqk = lqk_ref[...]
    s1 = jnp.sum(lqk[0:1] * lqk[1:2], axis=-1, keepdims=True)
    s2 = jnp.sum(lqk[2:3] * lqk[3:4], axis=-1, keepdims=True)
    return jnp.exp(s1) - jnp.exp(s2) + lam_init


def _diff_merge(o, lam, subln, lam_init):
    half = o.shape[0] // 2
    a = o[:half] - lam * o[half:]
    a = a * lax.rsqrt(jnp.mean(a * a, axis=-1, keepdims=True) + NORM_EPS) * subln
    return a * (1.0 - lam_init)


def _flash_kernel(*refs, mode, tq, tk, nrep, lam_init):
    if mode == "diff":
        q_ref, k_ref, v_ref, lqk_ref, subln_ref, o_ref, m_ref, l_ref, acc_ref = refs
    else:
        q_ref, k_ref, v_ref, wuv_ref, o_ref, m_ref, l_ref, acc_ref = refs
    qi = pl.program_id(1)
    ki = pl.program_id(2)
    nkv = pl.num_programs(2)
    rows = nrep * tq

    @pl.when(ki == 0)
    def _():
        m_ref[...] = jnp.full(m_ref.shape, NEG_INF, F32)
        l_ref[...] = jnp.zeros(l_ref.shape, F32)
        acc_ref[...] = jnp.zeros(acc_ref.shape, F32)

    @pl.when(ki * tk <= qi * tq + (tq - 1))
    def _():
        q = q_ref[0].reshape(rows, q_ref.shape[-1]).astype(BF16)
        k = k_ref[0].astype(BF16)
        s = lax.dot_general(q, k, (((1,), (1,)), ((), ())), preferred_element_type=F32)
        qpos = lax.broadcasted_iota(jnp.int32, (rows, tk), 0) % tq + qi * tq
        kpos = lax.broadcasted_iota(jnp.int32, (rows, tk), 1) + ki * tk
        s = jnp.where(kpos <= qpos, s, NEG_INF)
        m_prev = m_ref[...]
        m_new = jnp.maximum(m_prev, jnp.max(s, axis=-1, keepdims=True))
        corr = jnp.exp(m_prev - m_new)
        p = jnp.exp(s - m_new)
        l_ref[...] = l_ref[...] * corr + jnp.sum(p, axis=-1, keepdims=True)
        acc_ref[...] = acc_ref[...] * corr + jnp.dot(
            p.astype(BF16), v_ref[0].astype(BF16), preferred_element_type=F32)
        m_ref[...] = m_new

    @pl.when(ki == nkv - 1)
    def _():
        o = acc_ref[...] / l_ref[...]
        if mode == "diff":
            a = _diff_merge(o, _diff_lambda(lqk_ref, lam_init), subln_ref[...], lam_init)
            for hh in range(2):
                o_ref[0, :, hh * DVA:(hh + 1) * DVA] = a[hh * tq:(hh + 1) * tq].astype(o_ref.dtype)
        else:
            for h in range(nrep):
                y = jnp.dot(o[h * tq:(h + 1) * tq].astype(BF16), wuv_ref[h],
                            preferred_element_type=F32)
                o_ref[0, :, h * DV_C:(h + 1) * DV_C] = y.astype(o_ref.dtype)


def flash_prompt(mode, q, k, v, extras, *, tq, tk, lam_init=0.0, out_dtype=BF16):
    g, nrep, seq, d = q.shape
    tq = min(tq, seq)
    tk = min(tk, seq)
    nq, nkv = seq // tq, seq // tk

    def last_kv(qi):
        return (qi * tq + tq - 1) // tk

    if mode == "diff":
        b = g // KVH_A
        kv_map = lambda gi, qi, ki: (gi // KVH_A, jnp.minimum(ki, last_kv(qi)), gi % KVH_A)
        in_specs = [pl.BlockSpec((1, nrep, tq, d), lambda gi, qi, ki: (gi, 0, qi, 0)),
                    pl.BlockSpec((1, tk, 2 * DA), kv_map),
                    pl.BlockSpec((1, tk, DVA), kv_map),
                    pl.BlockSpec((4, DA), lambda gi, qi, ki: (0, 0)),
                    pl.BlockSpec((1, DVA), lambda gi, qi, ki: (0, 0))]
        out_spec = pl.BlockSpec((1, tq, 2 * DVA), lambda gi, qi, ki: (gi // KVH_A, qi, gi % KVH_A))
        dv = DVA
    else:
        b = g
        kv_map = lambda gi, qi, ki: (gi, jnp.minimum(ki, last_kv(qi)), 0)
        in_specs = [pl.BlockSpec((1, nrep, tq, d), lambda gi, qi, ki: (gi, 0, qi, 0)),
                    pl.BlockSpec((1, tk, d), kv_map),
                    pl.BlockSpec((1, tk, KR_C), kv_map),
                    pl.BlockSpec((H_C, KR_C, DV_C), lambda gi, qi, ki: (0, 0, 0))]
        out_spec = pl.BlockSpec((1, tq, GROUP_W), lambda gi, qi, ki: (gi, qi, 0))
        dv = KR_C
    rows = nrep * tq
    return pl.pallas_call(
        functools.partial(_flash_kernel, mode=mode, tq=tq, tk=tk, nrep=nrep, lam_init=lam_init),
        grid=(g, nq, nkv),
        in_specs=in_specs,
        out_specs=out_spec,
        out_shape=jax.ShapeDtypeStruct((b, seq, GROUP_W), out_dtype),
        scratch_shapes=[pltpu.VMEM((rows, 1), F32), pltpu.VMEM((rows, 1), F32),
                        pltpu.VMEM((rows, dv), F32)],
        compiler_params=_cparams(("parallel", "parallel", "arbitrary")),
    )(q, k, v, *extras)


PAGES_PER_STEP = 16


def _paged_kernel(*refs, mode, nq, nchunk, ppc, lam_init):
    pt_ref = refs[0]
    if mode == "diff":
        (q_ref, knew_ref, vnew_ref, lqk_ref, subln_ref, kc_hbm, vc_hbm,
         o_ref, kbuf, vbuf, sem, m_ref, l_ref, acc_ref) = refs[1:]
        ngrp = KVH_A
    else:
        (q_ref, knew_ref, vnew_ref, wuv_ref, kc_hbm, vc_hbm,
         o_ref, kbuf, vbuf, sem, m_ref, l_ref, acc_ref) = refs[1:]
        ngrp = 1
    b = pl.program_id(0)
    c = pl.program_id(1)
    nb = pl.num_programs(0)
    step = b * nchunk + c
    npages = nchunk * ppc
    rows = q_ref.shape[2]
    chunk_tok = ppc * PAGE_SIZE

    def copies(step_idx, slot):
        out = []
        for i in range(ppc):
            page = pt_ref[step_idx * ppc + i]
            out.append(pltpu.make_async_copy(kc_hbm.at[page], kbuf.at[slot, i], sem.at[0, slot]))
            out.append(pltpu.make_async_copy(vc_hbm.at[page], vbuf.at[slot, i], sem.at[1, slot]))
        return out

    @pl.when(step == 0)
    def _():
        for cp in copies(0, 0):
            cp.start()

    @pl.when(step + 1 < nb * nchunk)
    def _():
        for cp in copies(step + 1, (step + 1) % 2):
            cp.start()

    @pl.when(c == 0)
    def _():
        m_ref[...] = jnp.full(m_ref.shape, NEG_INF, F32)
        l_ref[...] = jnp.zeros(l_ref.shape, F32)
        acc_ref[...] = jnp.zeros(acc_ref.shape, F32)

    slot = step % 2
    for cp in copies(step, slot):
        cp.wait()

    def online(gi, s, pv_fn):
        m_prev = m_ref[gi]
        m_new = jnp.maximum(m_prev, jnp.max(s, axis=-1, keepdims=True))
        corr = jnp.exp(m_prev - m_new)
        p = jnp.exp(s - m_new)
        l_ref[gi] = l_ref[gi] * corr + jnp.sum(p, axis=-1, keepdims=True)
        acc_ref[gi] = acc_ref[gi] * corr + pv_fn(p)
        m_ref[gi] = m_new

    for gi in range(ngrp):
        q = q_ref[0, gi].astype(BF16)
        if mode == "diff":
            kc = kbuf[slot, :, :, gi * 2 * DA:(gi + 1) * 2 * DA].reshape(chunk_tok, 2 * DA)
            vc = vbuf[slot, :, :, gi * DVA:(gi + 1) * DVA].reshape(chunk_tok, DVA)
            s = lax.dot_general(q, kc.astype(BF16), (((1,), (1,)), ((), ())),
                                preferred_element_type=F32)
        else:
            kc = kbuf[slot].reshape(chunk_tok, KR_C)
            rc = vbuf[slot].reshape(chunk_tok, DR_C)
            vc = kc
            s = lax.dot_general(q[:, :KR_C], kc.astype(BF16), (((1,), (1,)), ((), ())),
                                preferred_element_type=F32)
            s = s + lax.dot_general(q[:, KR_C:], rc.astype(BF16), (((1,), (1,)), ((), ())),
                                    preferred_element_type=F32)
        vcb = vc.astype(BF16)
        online(gi, s, lambda p: jnp.dot(p.astype(BF16), vcb, preferred_element_type=F32))

    @pl.when(c == nchunk - 1)
    def _():
        outs = []
        for gi in range(ngrp):
            qf = q_ref[0, gi]
            tok = lax.broadcasted_iota(jnp.int32, (rows, 1), 0) % nq
            if mode == "diff":
                kn = knew_ref[0, :, gi * 2 * DA:(gi + 1) * 2 * DA]
                vn = vnew_ref[0, :, gi * DVA:(gi + 1) * DVA]
            else:
                kn = knew_ref[0]
                vn = vnew_ref[0]
            cols = [jnp.where(tok >= j,
                              jnp.sum(qf * kn[j:j + 1], axis=-1, keepdims=True), NEG_INF)
                    for j in range(nq)]
            m_prev = m_ref[gi]
            m_new = m_prev
            for sj in cols:
                m_new = jnp.maximum(m_new, sj)
            corr = jnp.exp(m_prev - m_new)
            l_new = l_ref[gi] * corr
            acc = acc_ref[gi] * corr
            for j, sj in enumerate(cols):
                pj = jnp.exp(sj - m_new)
                l_new = l_new + pj
                acc = acc + pj * vn[j:j + 1]
            outs.append(acc / l_new)
        if mode == "diff":
            lam = _diff_lambda(lqk_ref, lam_init)
            for gi in range(ngrp):
                o_ref[0, gi] = _diff_merge(outs[gi], lam, subln_ref[...], lam_init).astype(o_ref.dtype)
        else:
            y = jnp.dot(outs[0].astype(BF16), wuv_ref[...], preferred_element_type=F32)
            for h in range(H_C):
                o_ref[0, h] = y[h * nq:(h + 1) * nq, h * DV_C:(h + 1) * DV_C].astype(o_ref.dtype)


def paged_decode(mode, q, page_table, kc, vc, knew, vnew, extras, *, lam_init=0.0,
                 ppc=None, out_dtype=BF16):
    bsz, ngrp, rows, d = q.shape
    npages = page_table.shape[1]
    ppc = min(ppc or PAGES_PER_STEP, npages)
    nchunk = npages // ppc
    nq = knew.shape[1]
    pt = page_table.reshape(-1).astype(jnp.int32)
    qmap = lambda b, c, pt_ref: (b, 0, 0, 0)
    nmap = lambda b, c, pt_ref: (b, 0, 0)
    in_specs = [pl.BlockSpec((1, ngrp, rows, d), qmap),
                pl.BlockSpec((1, nq, knew.shape[2]), nmap),
                pl.BlockSpec((1, nq, vnew.shape[2]), nmap)]
    if mode == "diff":
        in_specs += [pl.BlockSpec((4, DA), lambda b, c, pt_ref: (0, 0)),
                     pl.BlockSpec((1, DVA), lambda b, c, pt_ref: (0, 0))]
        out_shape = (bsz, KVH_A, 2 * nq, DVA)
        dv = DVA
    else:
        in_specs += [pl.BlockSpec((KR_C, H_C * DV_C), lambda b, c, pt_ref: (0, 0))]
        out_shape = (bsz, H_C, nq, DV_C)
        dv = KR_C
    in_specs += [pl.BlockSpec(memory_space=pl.ANY), pl.BlockSpec(memory_space=pl.ANY)]
    grid_spec = pltpu.PrefetchScalarGridSpec(
        num_scalar_prefetch=1,
        grid=(bsz, nchunk),
        in_specs=in_specs,
        out_specs=pl.BlockSpec((1,) + out_shape[1:], lambda b, c, pt_ref: (b, 0, 0, 0)),
        scratch_shapes=[pltpu.VMEM((2, ppc) + kc.shape[1:], kc.dtype),
                        pltpu.VMEM((2, ppc) + vc.shape[1:], vc.dtype),
                        pltpu.SemaphoreType.DMA((2, 2)),
                        pltpu.VMEM((ngrp, rows, 1), F32),
                        pltpu.VMEM((ngrp, rows, 1), F32),
                        pltpu.VMEM((ngrp, rows, dv), F32)])
    return pl.pallas_call(
        functools.partial(_paged_kernel, mode=mode, nq=nq, nchunk=nchunk, ppc=ppc,
                          lam_init=lam_init),
        grid_spec=grid_spec,
        out_shape=jax.ShapeDtypeStruct(out_shape, out_dtype),
        compiler_params=_cparams(("arbitrary", "arbitrary")),
    )(pt, q, knew, vnew, *extras, kc, vc)


def _cumsum_rows(x):
    n = x.shape[0]
    row = lax.broadcasted_iota(jnp.int32, x.shape, 0)
    s = 1
    while s < n:
        x = x + jnp.where(row >= s, pltpu.roll(x, s, axis=0), 0.0)
        s *= 2
    return x


def _hgrn_kernel(q_ref, f_ref, i_ref, g_ref, lb_ref, gn_ref, s0_ref, y_ref, sout_ref,
                 st_ref, *, chunk):
    c = pl.program_id(1)
    nc = pl.num_programs(1)
    sub = 8

    @pl.when(c == 0)
    def _():
        for h in range(H_B):
            st_ref[h] = s0_ref[0, h].T

    for h in range(H_B):
        hs = slice(h * DK_B, (h + 1) * DK_B)
        lb = lb_ref[:, hs]
        qr = q_ref[0, :, hs]
        q = qr * jax.nn.sigmoid(qr)
        fr = f_ref[0, :, hs]
        f = lb + (1.0 - lb) * jax.nn.sigmoid(fr)
        cum = _cumsum_rows(jnp.log(jnp.maximum(f, F_MIN)))
        k = (1.0 - lb) * jax.nn.sigmoid(-fr)
        v = i_ref[0, :, hs]
        st = st_ref[h]
        last = cum[chunk - 1:chunk]
        o_inter = lax.dot_general((q * jnp.exp(cum)).astype(BF16), st.astype(BF16),
                                  (((1,), (1,)), ((), ())), preferred_element_type=F32)
        kd = (k * jnp.exp(last - cum)).astype(BF16)
        st_ref[h] = st * jnp.exp(last) + lax.dot_general(
            v.astype(BF16), kd, (((0,), (0,)), ((), ())), preferred_element_type=F32)
        o_rows = []
        for ib in range(chunk // sub):
            hi = (ib + 1) * sub
            cj = cum[:hi]
            kj = k[:hi]
            vj = v[:hi]
            jrow = lax.broadcasted_iota(jnp.int32, (hi, 1), 0)
            for i in range(ib * sub, hi):
                t = q[i:i + 1] * kj * jnp.exp(jnp.minimum(cum[i:i + 1] - cj, 0.0))
                att = jnp.where(jrow <= i, jnp.sum(t, axis=-1, keepdims=True), 0.0)
                o_rows.append(jnp.sum(att * vj, axis=0, keepdims=True))
        o = jnp.concatenate(o_rows, axis=0) + o_inter
        o = o * lax.rsqrt(jnp.mean(o * o, axis=-1, keepdims=True) + NORM_EPS) * gn_ref[...]
        gr = g_ref[0, :, hs]
        y_ref[0, :, hs] = (o * (gr * jax.nn.sigmoid(gr))).astype(y_ref.dtype)

    @pl.when(c == nc - 1)
    def _():
        for h in range(H_B):
            sout_ref[0, h] = st_ref[h].T


def hgrn_scan(pb, lb, gnorm, s0, *, chunk, out_dtype=BF16):
    bsz, seq, _ = pb.shape
    chunk = min(chunk, seq)
    wide = HB_K

    def field(j):
        return pl.BlockSpec((1, chunk, wide), lambda b, c, j=j: (b, c, j))

    return pl.pallas_call(
        functools.partial(_hgrn_kernel, chunk=chunk),
        grid=(bsz, seq // chunk),
        in_specs=[field(0), field(1), field(2), field(3),
                  pl.BlockSpec((1, HB_K), lambda b, c: (0, 0)),
                  pl.BlockSpec((1, DV_B), lambda b, c: (0, 0)),
                  pl.BlockSpec((1, H_B, DK_B, DV_B), lambda b, c: (b, 0, 0, 0))],
        out_specs=[pl.BlockSpec((1, chunk, GROUP_W), lambda b, c: (b, c, 0)),
                   pl.BlockSpec((1, H_B, DK_B, DV_B), lambda b, c: (b, 0, 0, 0))],
        out_shape=[jax.ShapeDtypeStruct((bsz, seq, GROUP_W), out_dtype),
                   jax.ShapeDtypeStruct((bsz, H_B, DK_B, DV_B), F32)],
        scratch_shapes=[pltpu.VMEM((H_B, DV_B, DK_B), F32)],
        compiler_params=_cparams(("parallel", "arbitrary")),
    )(pb, pb, pb, pb, lb.reshape(1, HB_K).astype(F32), gnorm.reshape(1, DV_B).astype(F32), s0)


WKV_SLOTS = 64


def _wkv_kernel(r_ref, w_ref, k_ref, a_ref, b_ref, v_ref, s0_ref, y_ref, sout_ref, st_ref, *, ntok):
    t0 = pl.program_id(1)
    nt = pl.num_programs(1)
    npair = H_D // 2

    @pl.when(t0 == 0)
    def _():
        st_ref[...] = s0_ref[0]

    lane = lax.broadcasted_iota(jnp.int32, (N_D, LANES), 1)
    base = (lane // WKV_SLOTS) * WKV_SLOTS

    sub = 8
    ntile = -(-ntok // sub)

    def body(tb, states):
        r0 = pl.multiple_of(tb * sub, sub)
        vt = v_ref[0, pl.ds(r0, sub), :]
        states = list(states)
        yrows = [[] for _ in range(npair)]
        for j in range(min(sub, ntok)):
            idx = base + (tb * sub + j)
            for p in range(npair):
                st = states[p]
                col = lambda ref: jnp.take_along_axis(ref[0, p], idx, axis=1)
                vrow = vt[j:j + 1, p * LANES:(p + 1) * LANES]
                sa = jnp.sum(st * col(a_ref), axis=0, keepdims=True)
                st = st * col(w_ref) + col(b_ref) * sa + col(k_ref) * vrow
                yrows[p].append(jnp.sum(st * col(r_ref), axis=0, keepdims=True))
                states[p] = st
        pad = [jnp.zeros((sub - len(yrows[0]), LANES), F32)] if len(yrows[0]) < sub else []
        y_ref[0, pl.ds(r0, sub), :] = jnp.concatenate(
            [jnp.concatenate(yrows[p] + pad, axis=0) for p in range(npair)], axis=1)
        return tuple(states)

    states = lax.fori_loop(0, ntile, body, tuple(st_ref[p] for p in range(npair)))
    for p in range(npair):
        st_ref[p] = states[p]

    @pl.when(t0 == nt - 1)
    def _():
        sout_ref[0] = st_ref[...]


def wkv_scan(r, decay, k, a, b, v, s0):
    bsz, seq, _ = r.shape
    npair = H_D // 2
    ntok = min(WKV_SLOTS, seq)
    nblk = seq // ntok

    def cols(x):
        x = x.reshape(bsz, nblk, ntok, npair, 2, N_D)
        x = jnp.transpose(x, (0, 3, 5, 1, 4, 2))
        if ntok < WKV_SLOTS:
            x = jnp.pad(x, ((0, 0),) * 5 + ((0, WKV_SLOTS - ntok),))
        return x.reshape(bsz, npair, N_D, nblk * LANES)

    st0 = jnp.transpose(s0.reshape(bsz, npair, 2, N_D, N_D), (0, 1, 4, 2, 3)).reshape(
        bsz, npair, N_D, LANES)
    nrow = max(ntok, 8)
    if nrow > ntok:
        v = jnp.pad(v, ((0, 0), (0, nrow - ntok), (0, 0)))
    col_spec = pl.BlockSpec((1, npair, N_D, LANES), lambda bi, ti: (bi, 0, 0, ti))
    row_spec = pl.BlockSpec((1, nrow, GROUP_W), lambda bi, ti: (bi, ti, 0))
    st_spec = pl.BlockSpec((1, npair, N_D, LANES), lambda bi, ti: (bi, 0, 0, 0))
    y, st = pl.pallas_call(
        functools.partial(_wkv_kernel, ntok=ntok),
        grid=(bsz, nblk),
        in_specs=[col_spec] * 5 + [row_spec, st_spec],
        out_specs=[row_spec, st_spec],
        out_shape=[jax.ShapeDtypeStruct((bsz, nblk * nrow, GROUP_W), F32),
                   jax.ShapeDtypeStruct((bsz, npair, N_D, LANES), F32)],
        scratch_shapes=[pltpu.VMEM((npair, N_D, LANES), F32)],
        compiler_params=_cparams(("parallel", "arbitrary")),
    )(cols(r), cols(decay), cols(k), cols(a), cols(b), v, st0)
    y = y[:, :seq]
    s = jnp.transpose(st.reshape(bsz, npair, N_D, 2, N_D), (0, 1, 3, 4, 2)).reshape(
        bsz, H_D, N_D, N_D)
    return y, s


def _rope(x, pos):
    d = x.shape[-1]
    inv_freq = ROPE_THETA ** (-jnp.arange(0, d, 2, dtype=F32) / d)
    ang = pos.astype(F32)[:, None] * inv_freq[None, :]
    cos = jnp.cos(ang)[:, None, :]
    sin = jnp.sin(ang)[:, None, :]
    x1, x2 = x[..., : d // 2], x[..., d // 2:]
    return jnp.concatenate([x1 * cos - x2 * sin, x2 * cos + x1 * sin], axis=-1)


def _diff_queries(q, bsz, seq):
    q = q.reshape(bsz, seq, KVH_A, 2, 2, DA) * (DA ** -0.5)
    q = jnp.transpose(q, (0, 2, 4, 3, 1, 5))
    z = jnp.zeros_like(q[:, :, 0])
    q0 = jnp.concatenate([q[:, :, 0], z], axis=-1)
    q1 = jnp.concatenate([z, q[:, :, 1]], axis=-1)
    return jnp.concatenate([q0, q1], axis=2)


def _layer(x, n_prompt, bp, sp, bs, ss, pos_p, pos_s, layer_idx, lp, caches, states, page_table):
    rows = x.shape[0]
    h = rms_rows(x, lp["g_pre_mix"], BF16)
    w_in = lp["w_in"]
    a1 = A_COLS
    b1 = a1 + B_COLS
    c1 = b1 + C_COLS
    wa = w_in[:, :a1].astype(BF16)
    wb = w_in[:, a1:b1].astype(BF16)
    wc = jnp.pad(w_in[:, b1:c1], ((0, 0), (0, 640 - C_COLS))).astype(BF16)
    wd = jnp.pad(w_in[:, c1:], ((0, 0), (0, 1792 - D_COLS))).astype(BF16)
    pa = matmul(h, wa, tn=512)
    pb = matmul(h, wb, tn=1024)
    pc = matmul(h, wc, tn=640)[:, :C_COLS]
    pd = matmul(h, wd, tn=896)[:, :D_COLS]

    def split(t):
        return t[:n_prompt].reshape(bp, sp, -1), t[n_prompt:].reshape(bs, ss, -1)

    lam_init = 0.8 - 0.6 * math.exp(-0.3 * layer_idx)
    lqk = jnp.stack([lp["diff_lq1"], lp["diff_lk1"], lp["diff_lq2"], lp["diff_lk2"]]).astype(F32)
    subln = lp["diff_subln"].reshape(1, DVA).astype(F32)

    outs = []
    pa_p, pa_s = split(pa)
    res_a = []
    for pa_g, pos, is_prompt in ((pa_p, pos_p, True), (pa_s, pos_s, False)):
        b, l, _ = pa_g.shape
        q = _rope(pa_g[..., :A_Q].reshape(b, l, 2 * H_A, DA), pos)
        k_rows = _rope(pa_g[..., A_Q:A_Q + A_K].reshape(b, l, 2 * KVH_A, DA), pos).reshape(
            b, l, KVH_A * 2 * DA)
        v_rows = pa_g[..., A_Q + A_K:]
        qd = _diff_queries(q, b, l)
        if is_prompt:
            ya = flash_prompt("diff", qd.reshape(b * KVH_A, 4, l, 2 * DA), k_rows, v_rows,
                              (lqk, subln), tq=256, tk=512, lam_init=lam_init)
        else:
            o = paged_decode("diff", qd.reshape(b, KVH_A, 4 * l, 2 * DA), page_table,
                             caches["diff_k"], caches["diff_v"], k_rows, v_rows,
                             (lqk, subln), lam_init=lam_init)
            ya = jnp.transpose(o.reshape(b, KVH_A, 2, l, DVA), (0, 3, 1, 2, 4)).reshape(b, l, GROUP_W)
        res_a.append((ya, k_rows.reshape(b, l, KVH_A, 2 * DA), v_rows.reshape(b, l, KVH_A, DVA)))

    pb_p, pb_s = split(pb)
    yb_p, sh_p = hgrn_scan(pb_p, lp["hgrn_lb"], lp["hgrn_norm"],
                           jnp.zeros((bp, H_B, DK_B, DV_B), F32), chunk=64)
    pad_s = 8 - ss
    pb_s_pad = jnp.pad(pb_s, ((0, 0), (0, pad_s), (0, 0)))
    big = jnp.full((bs, pad_s, HB_K), 1e4, F32)
    pb_s_pad = pb_s_pad.at[:, ss:, HB_K:2 * HB_K].set(big)
    yb_s, sh_s = hgrn_scan(pb_s_pad, lp["hgrn_lb"], lp["hgrn_norm"], states["hgrn"], chunk=8)
    yb_s = yb_s[:, :ss]

    pc_p, pc_s = split(pc)
    cq = rms_rows(pc[:, :QR_C], lp["mla_q_norm"], BF16)
    w_uq = lp["mla_w_uq"].reshape(QR_C, H_C * (DN_C + DR_C)).astype(BF16)
    qfull = matmul(cq, w_uq, tn=H_C * (DN_C + DR_C)).reshape(rows, H_C, DN_C + DR_C)
    w_uk = lp["mla_w_uk"]
    wabs = jnp.zeros((H_C, DN_C, H_C, KR_C), F32)
    for hh in range(H_C):
        wabs = wabs.at[hh, :, hh, :].set(w_uk[:, hh, :].T)
    q_lat = matmul(qfull[:, :, :DN_C].reshape(rows, H_C * DN_C).astype(BF16),
                   wabs.reshape(H_C * DN_C, H_C * KR_C).astype(BF16), tn=H_C * KR_C)
    q_lat = q_lat.reshape(rows, H_C, KR_C)
    c_kv = rms_rows(pc[:, QR_C:QR_C + KR_C], lp["mla_kv_norm"], F32)
    w_uv = jnp.transpose(lp["mla_w_uv"], (1, 0, 2)).astype(BF16)
    res_c = []
    scale_c = (DN_C + DR_C) ** -0.5
    for sl, b, l, pos, is_prompt in ((slice(0, n_prompt), bp, sp, pos_p, True),
                                     (slice(n_prompt, rows), bs, ss, pos_s, False)):
        q_rope = _rope(qfull[sl, :, DN_C:].reshape(b, l, H_C, DR_C), pos)
        qc = jnp.concatenate([q_lat[sl].reshape(b, l, H_C, KR_C), q_rope], axis=-1) * scale_c
        qc = jnp.transpose(qc, (0, 2, 1, 3))
        ckv = c_kv[sl].reshape(b, l, KR_C)
        k_rope = _rope(pc[sl, QR_C + KR_C:].reshape(b, l, 1, DR_C), pos).reshape(b, l, DR_C)
        kcat = jnp.concatenate([ckv, k_rope], axis=-1)
        if is_prompt:
            yc = flash_prompt("mla", qc, kcat, ckv, (w_uv,), tq=256, tk=512)
        else:
            wuv_flat = jnp.transpose(w_uv, (1, 0, 2)).reshape(KR_C, H_C * DV_C)
            o = paged_decode("mla", qc.reshape(b, 1, H_C * l, KR_C + DR_C), page_table,
                             caches["mla_latent"], caches["mla_rope"], kcat, ckv, (wuv_flat,))
            yc = jnp.transpose(o, (0, 2, 1, 3)).reshape(b, l, GROUP_W)
        res_c.append((yc, ckv, k_rope))

    pd_p, pd_s = split(pd)
    res_d = []
    for pd_g, shift_prev, s0 in ((pd_p, jnp.zeros((bp, D_COLS), F32), jnp.zeros((bp, H_D, N_D, N_D), F32)),
                                 (pd_s, states["shift"], states["rwkv"])):
        b, l, _ = pd_g.shape
        G = GROUP_W
        prev = jnp.concatenate([shift_prev[:, None, :], pd_g[:, :-1]], axis=1)
        xm = pd_g + (prev - pd_g) * lp["rwkv_mu"]
        r, k, v = xm[..., :G], xm[..., G:2 * G], xm[..., 2 * G:3 * G]
        o0 = 3 * G
        wdl = xm[..., o0:o0 + W_LORA]
        adl = xm[..., o0 + W_LORA:o0 + W_LORA + A_LORA]
        gdl = xm[..., o0 + W_LORA + A_LORA:]
        lora_in = jnp.concatenate([jnp.tanh(wdl), adl, jax.nn.sigmoid(gdl)], axis=-1)
        lora_in = jnp.pad(lora_in.reshape(b * l, 160), ((0, 0), (0, 96)))
        wl = jnp.zeros((256, 3 * G), F32)
        wl = wl.at[0:W_LORA, 0:G].set(lp["rwkv_w2"])
        wl = wl.at[W_LORA:W_LORA + A_LORA, G:2 * G].set(lp["rwkv_a2"])
        wl = wl.at[W_LORA + A_LORA:160, 2 * G:].set(lp["rwkv_g2"])
        lo = matmul(lora_in, wl.astype(BF16), tn=3 * G).reshape(b, l, 3 * G)
        w = -jax.nn.softplus(-(lp["rwkv_w0"] + lo[..., :G])) - 0.5
        decay = jnp.exp(-jnp.exp(w))
        a = jax.nn.sigmoid(lp["rwkv_a0"] + lo[..., G:2 * G])
        g = lo[..., 2 * G:]
        kk = (k * lp["rwkv_kk"]).reshape(b, l, H_D, N_D)
        kk = kk / jnp.maximum(jnp.sqrt(jnp.sum(kk * kk, axis=-1, keepdims=True)), 1e-12)
        kk = kk.reshape(b, l, G)
        k2 = k * (1.0 + (a - 1.0) * lp["rwkv_ka"])
        y, s_new = wkv_scan(r, decay, k2, -kk, kk * a, v, s0)
        y4 = y.reshape(b, l, H_D, N_D)
        mu = jnp.mean(y4, axis=-1, keepdims=True)
        var = jnp.mean(jnp.square(y4 - mu), axis=-1, keepdims=True)
        yn = ((y4 - mu) * lax.rsqrt(var + RWKV_LN_EPS)).reshape(b, l, G)
        yn = yn * lp["rwkv_ln_w"] + lp["rwkv_ln_b"]
        bonus = jnp.sum((r * k2).reshape(b, l, H_D, N_D) * lp["rwkv_rk"], axis=-1, keepdims=True)
        bonus = (bonus * v.reshape(b, l, H_D, N_D)).reshape(b, l, G)
        yd = ((yn + bonus) * g).astype(BF16)
        res_d.append((yd, s_new, pd_g[:, -1]))

    mix_p = jnp.concatenate([res_a[0][0], yb_p, res_c[0][0], res_d[0][0]], axis=-1).reshape(n_prompt, D_MODEL)
    mix_s = jnp.concatenate([res_a[1][0], yb_s, res_c[1][0], res_d[1][0]], axis=-1).reshape(rows - n_prompt, D_MODEL)
    mix = jnp.concatenate([mix_p, mix_s], axis=0)
    x = matmul(mix, lp["w_out"].astype(BF16), tn=D_MODEL, tk=1024, post=(lp["g_post_mix"], x))
    hf = rms_rows(x, lp["g_pre_ffn"], BF16)
    u = matmul(hf, lp["w_up"].astype(BF16), tn=1024, out_dtype=BF16, act="relu2")
    x = matmul(u, lp["w_down"].astype(BF16), tn=D_MODEL, tk=1024, post=(lp["g_post_ffn"], x))
    rows_p = (res_a[0][1], res_a[0][2], res_c[0][1], res_c[0][2], sh_p, res_d[0][1], res_d[0][2])
    rows_s = (res_a[1][1], res_a[1][2], res_c[1][1], res_c[1][2], sh_s, res_d[1][1], res_d[1][2])
    return x, rows_p, rows_s


def kernel(x_prompt, x_sample, cache_diff_k, cache_diff_v, cache_mla_latent, cache_mla_rope, state_hgrn, state_rwkv, state_rwkv_shift, page_table, hgrn_lb_logits, g_pre_mix, w_in, w_out, g_post_mix, diff_lq1, diff_lk1, diff_lq2, diff_lk2, diff_subln, hgrn_norm, mla_q_norm, mla_w_uq, mla_kv_norm, mla_w_uk, mla_w_uv, rwkv_mu, rwkv_w0, rwkv_w2, rwkv_a0, rwkv_a2, rwkv_g2, rwkv_kk, rwkv_ka, rwkv_rk, rwkv_ln_w, rwkv_ln_b, g_pre_ffn, w_up, w_down, g_post_ffn):
    depth = w_in.shape[0]
    probs = jax.nn.softmax(hgrn_lb_logits.astype(F32), axis=0)
    lower_bounds = jnp.clip(jnp.cumsum(probs, axis=0) - probs[0:1], 0.0, 1.0)
    bp, sp, _ = x_prompt.shape
    bs, ss, _ = x_sample.shape
    past_len = page_table.shape[1] * PAGE_SIZE
    pos_p = jnp.arange(sp)
    pos_s = past_len + jnp.arange(ss)
    n_prompt = bp * sp
    x = jnp.concatenate([x_prompt.reshape(n_prompt, D_MODEL), x_sample.reshape(bs * ss, D_MODEL)], axis=0)
    pool = cache_diff_k.shape[1]
    caches = dict(diff_k=cache_diff_k.reshape(depth * pool, PAGE_SIZE, KVH_A * 2 * DA),
                  diff_v=cache_diff_v.reshape(depth * pool, PAGE_SIZE, KVH_A * DVA),
                  mla_latent=cache_mla_latent.reshape(depth * pool, PAGE_SIZE, KR_C),
                  mla_rope=cache_mla_rope.reshape(depth * pool, PAGE_SIZE, DR_C))
    rows_p, rows_s = [], []
    for l in range(depth):
        lp = dict(
            g_pre_mix=g_pre_mix[l], w_in=w_in[l], w_out=w_out[l], g_post_mix=g_post_mix[l],
            diff_lq1=diff_lq1[l], diff_lk1=diff_lk1[l], diff_lq2=diff_lq2[l], diff_lk2=diff_lk2[l],
            diff_subln=diff_subln[l], hgrn_lb=lower_bounds[l], hgrn_norm=hgrn_norm[l],
            mla_q_norm=mla_q_norm[l], mla_w_uq=mla_w_uq[l], mla_kv_norm=mla_kv_norm[l],
            mla_w_uk=mla_w_uk[l], mla_w_uv=mla_w_uv[l],
            rwkv_mu=rwkv_mu[l], rwkv_w0=rwkv_w0[l], rwkv_w2=rwkv_w2[l], rwkv_a0=rwkv_a0[l],
            rwkv_a2=rwkv_a2[l], rwkv_g2=rwkv_g2[l], rwkv_kk=rwkv_kk[l], rwkv_ka=rwkv_ka[l],
            rwkv_rk=rwkv_rk[l], rwkv_ln_w=rwkv_ln_w[l], rwkv_ln_b=rwkv_ln_b[l],
            g_pre_ffn=g_pre_ffn[l], w_up=w_up[l], w_down=w_down[l], g_post_ffn=g_post_ffn[l])
        states = dict(hgrn=state_hgrn[l], rwkv=state_rwkv[l], shift=state_rwkv_shift[l])
        x, rp, rs = _layer(x, n_prompt, bp, sp, bs, ss, pos_p, pos_s, l, lp, caches, states,
                           page_table + l * pool)
        rows_p.append(rp)
        rows_s.append(rs)
    xp = x[:n_prompt].reshape(bp, sp, D_MODEL)
    xs = x[n_prompt:].reshape(bs, ss, D_MODEL)
    outs_p = [jnp.stack([r[i] for r in rows_p]) for i in range(7)]
    outs_s = [jnp.stack([r[i] for r in rows_s]) for i in range(7)]
    return (xp, xs, *outs_p, *outs_s)
```

```python
import functools
import math

import jax
import jax.numpy as jnp
from jax import lax
from jax.experimental import pallas as pl
from jax.experimental.pallas import tpu as pltpu

F32 = jnp.float32
BF16 = jnp.bfloat16

D_MODEL = 2048
PAGE_SIZE = 128
N_GROUPS = 4
GROUP_W = D_MODEL // N_GROUPS
D_FF = 4 * D_MODEL
ROPE_THETA = 10000.0
NORM_EPS = 1e-6
NEG_INF = -1e30
F_MIN = 1e-6

DA = 64
DVA = 2 * DA
H_A = GROUP_W // DVA
KVH_A = H_A // 2
DK_B = 128
DV_B = 128
H_B = GROUP_W // DV_B
HB_K = H_B * DK_B
DN_C = 128
DR_C = 32
DV_C = 128
H_C = GROUP_W // DV_C
QR_C = 384
KR_C = 128
N_D = 64
H_D = GROUP_W // N_D
W_LORA = 32
A_LORA = 32
G_LORA = 96
RWKV_LN_EPS = 64e-5

A_Q = 2 * H_A * DA
A_K = 2 * KVH_A * DA
A_V = KVH_A * DVA
A_COLS = A_Q + A_K + A_V
B_COLS = 2 * HB_K + 2 * GROUP_W
C_COLS = QR_C + KR_C + DR_C
D_COLS = 3 * GROUP_W + W_LORA + A_LORA + G_LORA

LANES = 128
VMEM_LIMIT = 48 * 1024 * 1024
ROW_TILE = 512


def _cparams(sem):
    return pltpu.CompilerParams(dimension_semantics=sem, vmem_limit_bytes=VMEM_LIMIT)


def _row_tile(m, cap=ROW_TILE):
    t = cap
    while m % t:
        t //= 2
    return t


def _rms_kernel(x_ref, g_ref, o_ref):
    x = x_ref[...].astype(F32)
    y = x * lax.rsqrt(jnp.mean(x * x, axis=-1, keepdims=True) + NORM_EPS)
    o_ref[...] = (y * g_ref[...]).astype(o_ref.dtype)


def rms_rows(x, g, out_dtype, tm=ROW_TILE):
    m, d = x.shape
    tm = _row_tile(m, tm)
    return pl.pallas_call(
        _rms_kernel,
        grid=(m // tm,),
        in_specs=[pl.BlockSpec((tm, d), lambda i: (i, 0)),
                  pl.BlockSpec((1, d), lambda i: (0, 0))],
        out_specs=pl.BlockSpec((tm, d), lambda i: (i, 0)),
        out_shape=jax.ShapeDtypeStruct((m, d), out_dtype),
        compiler_params=_cparams(("parallel",)),
    )(x, g.reshape(1, d).astype(F32))


def _mm_kernel(*refs, nk, act, post):
    if post:
        x_ref, w_ref, g_ref, res_ref, o_ref = refs[:5]
        scratch = refs[5:]
    else:
        x_ref, w_ref, o_ref = refs[:3]
        scratch = refs[3:]
    part = jnp.dot(x_ref[...].astype(BF16), w_ref[...].astype(BF16),
                   preferred_element_type=F32)

    def finish(acc):
        if act == "relu2":
            r = jnp.maximum(acc, 0.0)
            acc = r * r
        if post:
            y = acc * lax.rsqrt(jnp.mean(acc * acc, axis=-1, keepdims=True) + NORM_EPS)
            acc = res_ref[...] + y * g_ref[...]
        o_ref[...] = acc.astype(o_ref.dtype)

    if nk == 1:
        finish(part)
    else:
        acc_ref = scratch[0]
        k = pl.program_id(2)

        @pl.when(k == 0)
        def _():
            acc_ref[...] = part

        @pl.when(k > 0)
        def _():
            acc_ref[...] += part

        @pl.when(k == nk - 1)
        def _():
            finish(acc_ref[...])


def matmul(x, w, *, tn, tk=None, tm=ROW_TILE, out_dtype=F32, act=None, post=None):
    m, kdim = x.shape
    n = w.shape[1]
    tm = _row_tile(m, tm)
    tk = tk or kdim
    nk = kdim // tk
    assert m % tm == 0 and n % tn == 0 and kdim % tk == 0
    in_specs = [pl.BlockSpec((tm, tk), lambda i, j, k: (i, k)),
                pl.BlockSpec((tk, tn), lambda i, j, k: (k, j))]
    args = [x, w]
    if post:
        assert tn == n
        gain, res = post
        in_specs += [pl.BlockSpec((1, n), lambda i, j, k: (0, 0)),
                     pl.BlockSpec((tm, n), lambda i, j, k: (i, 0))]
        args += [gain.reshape(1, n).astype(F32), res]
    return pl.pallas_call(
        functools.partial(_mm_kernel, nk=nk, act=act, post=bool(post)),
        grid=(m // tm, n // tn, nk),
        in_specs=in_specs,
        out_specs=pl.BlockSpec((tm, tn), lambda i, j, k: (i, j)),
        out_shape=jax.ShapeDtypeStruct((m, n), out_dtype),
        scratch_shapes=[pltpu.VMEM((tm, tn), F32)] if nk > 1 else [],
        compiler_params=_cparams(("parallel", "parallel", "arbitrary")),
    )(*args)


def _diff_lambda(lqk_ref, lam_init):
    lqk = lqk_ref[...]
    s1 = jnp.sum(lqk[0:1] * lqk[1:2], axis=-1, keepdims=True)
    s2 = jnp.sum(lqk[2:3] * lqk[3:4], axis=-1, keepdims=True)
    return jnp.exp(s1) - jnp.exp(s2) + lam_init


def _diff_merge(o, lam, subln, lam_init):
    half = o.shape[0] // 2
    a = o[:half] - lam * o[half:]
    a = a * lax.rsqrt(jnp.mean(a * a, axis=-1, keepdims=True) + NORM_EPS) * subln
    return a * (1.0 - lam_init)


def _flash_kernel(*refs, mode, tq, tk, nrep, lam_init):
    if mode == "diff":
        q_ref, k_ref, v_ref, lqk_ref, subln_ref, o_ref, m_ref, l_ref, acc_ref = refs
    else:
        q_ref, k_ref, v_ref, wuv_ref, o_ref, m_ref, l_ref, acc_ref = refs
    qi = pl.program_id(1)
    ki = pl.program_id(2)
    nkv = pl.num_programs(2)
    rows = nrep * tq

    @pl.when(ki == 0)
    def _():
        m_ref[...] = jnp.full(m_ref.shape, NEG_INF, F32)
        l_ref[...] = jnp.zeros(l_ref.shape, F32)
        acc_ref[...] = jnp.zeros(acc_ref.shape, F32)

    @pl.when(ki * tk <= qi * tq + (tq - 1))
    def _():
        q = q_ref[0].reshape(rows, q_ref.shape[-1]).astype(BF16)
        k = k_ref[0].astype(BF16)
        s = lax.dot_general(q, k, (((1,), (1,)), ((), ())), preferred_element_type=F32)
        qpos = lax.broadcasted_iota(jnp.int32, (rows, tk), 0) % tq + qi * tq
        kpos = lax.broadcasted_iota(jnp.int32, (rows, tk), 1) + ki * tk
        s = jnp.where(kpos <= qpos, s, NEG_INF)
        m_prev = m_ref[...]
        m_new = jnp.maximum(m_prev, jnp.max(s, axis=-1, keepdims=True))
        corr = jnp.exp(m_prev - m_new)
        p = jnp.exp(s - m_new)
        l_ref[...] = l_ref[...] * corr + jnp.sum(p, axis=-1, keepdims=True)
        acc_ref[...] = acc_ref[...] * corr + jnp.dot(
            p.astype(BF16), v_ref[0].astype(BF16), preferred_element_type=F32)
        m_ref[...] = m_new

    @pl.when(ki == nkv - 1)
    def _():
        o = acc_ref[...] / l_ref[...]
        if mode == "diff":
            a = _diff_merge(o, _diff_lambda(lqk_ref, lam_init), subln_ref[...], lam_init)
            for hh in range(2):
                o_ref[0, :, hh * DVA:(hh + 1) * DVA] = a[hh * tq:(hh + 1) * tq].astype(o_ref.dtype)
        else:
            for h in range(nrep):
                y = jnp.dot(o[h * tq:(h + 1) * tq].astype(BF16), wuv_ref[h],
                            preferred_element_type=F32)
                o_ref[0, :, h * DV_C:(h + 1) * DV_C] = y.astype(o_ref.dtype)


def flash_prompt(mode, q, k, v, extras, *, tq, tk, lam_init=0.0, out_dtype=BF16):
    g, nrep, seq, d = q.shape
    tq = min(tq, seq)
    tk = min(tk, seq)
    nq, nkv = seq // tq, seq // tk

    def last_kv(qi):
        return (qi * tq + tq - 1) // tk

    if mode == "diff":
        b = g // KVH_A
        kv_map = lambda gi, qi, ki: (gi // KVH_A, jnp.minimum(ki, last_kv(qi)), gi % KVH_A)
        in_specs = [pl.BlockSpec((1, nrep, tq, d), lambda gi, qi, ki: (gi, 0, qi, 0)),
                    pl.BlockSpec((1, tk, 2 * DA), kv_map),
                    pl.BlockSpec((1, tk, DVA), kv_map),
                    pl.BlockSpec((4, DA), lambda gi, qi, ki: (0, 0)),
                    pl.BlockSpec((1, DVA), lambda gi, qi, ki: (0, 0))]
        out_spec = pl.BlockSpec((1, tq, 2 * DVA), lambda gi, qi, ki: (gi // KVH_A, qi, gi % KVH_A))
        dv = DVA
    else:
        b = g
        kv_map = lambda gi, qi, ki: (gi, jnp.minimum(ki, last_kv(qi)), 0)
        in_specs = [pl.BlockSpec((1, nrep, tq, d), lambda gi, qi, ki: (gi, 0, qi, 0)),
                    pl.BlockSpec((1, tk, d), kv_map),
                    pl.BlockSpec((1, tk, KR_C), kv_map),
                    pl.BlockSpec((H_C, KR_C, DV_C), lambda gi, qi, ki: (0, 0, 0))]
        out_spec = pl.BlockSpec((1, tq, GROUP_W), lambda gi, qi, ki: (gi, qi, 0))
        dv = KR_C
    rows = nrep * tq
    return pl.pallas_call(
        functools.partial(_flash_kernel, mode=mode, tq=tq, tk=tk, nrep=nrep, lam_init=lam_init),
        grid=(g, nq, nkv),
        in_specs=in_specs,
        out_specs=out_spec,
        out_shape=jax.ShapeDtypeStruct((b, seq, GROUP_W), out_dtype),
        scratch_shapes=[pltpu.VMEM((rows, 1), F32), pltpu.VMEM((rows, 1), F32),
                        pltpu.VMEM((rows, dv), F32)],
        compiler_params=_cparams(("parallel", "parallel", "arbitrary")),
    )(q, k, v, *extras)


PAGES_PER_STEP = 32


def _page_stream(pt_ref, kc_hbm, vc_hbm, kbuf, vbuf, sem, nchunk, ppc):
    step = pl.program_id(0) * nchunk + pl.program_id(1)
    nstep = pl.num_programs(0) * nchunk

    def copies(step_idx, slot):
        out = []
        for i in range(ppc):
            page = pt_ref[step_idx * ppc + i]
            out.append(pltpu.make_async_copy(kc_hbm.at[page], kbuf.at[slot, i], sem.at[0, slot]))
            out.append(pltpu.make_async_copy(vc_hbm.at[page], vbuf.at[slot, i], sem.at[1, slot]))
        return out

    @pl.when(step == 0)
    def _():
        for cp in copies(0, 0):
            cp.start()

    @pl.when(step + 1 < nstep)
    def _():
        for cp in copies(step + 1, (step + 1) % 2):
            cp.start()

    slot = step % 2
    for cp in copies(step, slot):
        cp.wait()
    return slot


def _softmax_parts(s, v):
    m = jnp.max(s, axis=-1, keepdims=True)
    p = jnp.exp(s - m)
    return m, jnp.sum(p, axis=-1, keepdims=True), jnp.dot(
        p.astype(BF16), v, preferred_element_type=F32)


def _merge_parts(m_ref, l_ref, acc_ref, parts):
    m_prev = m_ref[...]
    m_new = m_prev
    for m, _, _ in parts:
        m_new = jnp.maximum(m_new, m)
    corr = jnp.exp(m_prev - m_new)
    l_new = l_ref[...] * corr
    acc = acc_ref[...] * corr
    for m, l, a in parts:
        w = jnp.exp(m - m_new)
        l_new = l_new + l * w
        acc = acc + a * w
    m_ref[...] = m_new
    l_ref[...] = l_new
    acc_ref[...] = acc


def _new_tokens(qf, kn, vn, m_prev, l_prev, acc_prev, nq):
    tok = lax.broadcasted_iota(jnp.int32, (qf.shape[0], 1), 0) % nq
    cols = [jnp.where(tok >= j, jnp.sum(qf * kn[j:j + 1], axis=-1, keepdims=True), NEG_INF)
            for j in range(nq)]
    m_new = m_prev
    for sj in cols:
        m_new = jnp.maximum(m_new, sj)
    corr = jnp.exp(m_prev - m_new)
    l_new = l_prev * corr
    acc = acc_prev * corr
    for j, sj in enumerate(cols):
        pj = jnp.exp(sj - m_new)
        l_new = l_new + pj
        acc = acc + pj * vn[j:j + 1]
    return acc / l_new


def _paged_diff_kernel(pt_ref, q_ref, knew_ref, vnew_ref, lqk_ref, subln_ref, kc_hbm, vc_hbm,
                       o_ref, kbuf, vbuf, sem, m_ref, l_ref, acc_ref, *, nq, nchunk, ppc, lam_init):
    c = pl.program_id(1)
    slot = _page_stream(pt_ref, kc_hbm, vc_hbm, kbuf, vbuf, sem, nchunk, ppc)

    @pl.when(c == 0)
    def _():
        m_ref[...] = jnp.full(m_ref.shape, NEG_INF, F32)
        l_ref[...] = jnp.zeros(l_ref.shape, F32)
        acc_ref[...] = jnp.zeros(acc_ref.shape, F32)

    q = q_ref[0].astype(BF16)
    nrow = q.shape[0]
    rows = nrow // KVH_A
    half = ppc // 2
    hrows = half * PAGE_SIZE * KVH_A
    own = (lax.broadcasted_iota(jnp.int32, (nrow, hrows), 1) % KVH_A
           == lax.broadcasted_iota(jnp.int32, (nrow, hrows), 0) // rows)
    parts = []
    for hf in range(2):
        kc = kbuf[slot, hf * half:(hf + 1) * half].reshape(hrows, 2 * DA).astype(BF16)
        vc = vbuf[slot, hf * half:(hf + 1) * half].reshape(hrows, DVA).astype(BF16)
        s = lax.dot_general(q, kc, (((1,), (1,)), ((), ())), preferred_element_type=F32)
        parts.append(_softmax_parts(jnp.where(own, s, NEG_INF), vc))
    _merge_parts(m_ref, l_ref, acc_ref, parts)

    @pl.when(c == nchunk - 1)
    def _():
        lam = _diff_lambda(lqk_ref, lam_init)
        for gi in range(KVH_A):
            rs = slice(gi * rows, (gi + 1) * rows)
            o = _new_tokens(q_ref[0, rs], knew_ref[0, :, gi * 2 * DA:(gi + 1) * 2 * DA],
                            vnew_ref[0, :, gi * DVA:(gi + 1) * DVA],
                            m_ref[rs], l_ref[rs], acc_ref[rs], nq)
            o_ref[0, gi * rows // 2:(gi + 1) * rows // 2] = _diff_merge(
                o, lam, subln_ref[...], lam_init).astype(o_ref.dtype)


def _paged_mla_kernel(pt_ref, q_ref, knew_ref, vnew_ref, wuv_ref, kc_hbm, rc_hbm,
                      o_ref, kbuf, rbuf, sem, m_ref, l_ref, acc_ref, *, nq, nchunk, ppc):
    c = pl.program_id(1)
    slot = _page_stream(pt_ref, kc_hbm, rc_hbm, kbuf, rbuf, sem, nchunk, ppc)

    @pl.when(c == 0)
    def _():
        m_ref[...] = jnp.full(m_ref.shape, NEG_INF, F32)
        l_ref[...] = jnp.zeros(l_ref.shape, F32)
        acc_ref[...] = jnp.zeros(acc_ref.shape, F32)

    q = q_ref[0].astype(BF16)
    half = ppc // 2
    htok = half * PAGE_SIZE
    parts = []
    for hf in range(2):
        lat = kbuf[slot, hf * half:(hf + 1) * half].reshape(htok, KR_C).astype(BF16)
        s = lax.dot_general(q[:, :KR_C], lat, (((1,), (1,)), ((), ())), preferred_element_type=F32)
        s_rope = [jnp.dot(q[:, KR_C:], rbuf[slot, hf * half + i].astype(BF16),
                          preferred_element_type=F32) for i in range(half)]
        parts.append(_softmax_parts(s + jnp.concatenate(s_rope, axis=1), lat))
    _merge_parts(m_ref, l_ref, acc_ref, parts)

    @pl.when(c == nchunk - 1)
    def _():
        o = _new_tokens(q_ref[0], knew_ref[0], vnew_ref[0], m_ref[...], l_ref[...], acc_ref[...], nq)
        y = jnp.dot(o.astype(BF16), wuv_ref[...], preferred_element_type=F32)
        for h in range(H_C):
            o_ref[0, h] = y[h * nq:(h + 1) * nq, h * DV_C:(h + 1) * DV_C].astype(o_ref.dtype)


def _paged_call(body, q, page_table, kc, vc, knew, vnew, extras, out_shape, dv, out_dtype,
                page_mult=1):
    bsz, nrow, d = q.shape
    npages = page_table.shape[1]
    ppc = min(PAGES_PER_STEP * page_mult, npages)
    nchunk = npages // ppc
    nq = knew.shape[1]
    pt = page_table.reshape(-1).astype(jnp.int32)
    in_specs = [pl.BlockSpec((1, nrow, d), lambda b, c, pt_ref: (b, 0, 0)),
                pl.BlockSpec((1, nq, knew.shape[2]), lambda b, c, pt_ref: (b, 0, 0)),
                pl.BlockSpec((1, nq, vnew.shape[2]), lambda b, c, pt_ref: (b, 0, 0))]
    for e in extras:
        in_specs.append(pl.BlockSpec(e.shape, lambda b, c, pt_ref: (0, 0)))
    in_specs += [pl.BlockSpec(memory_space=pl.ANY), pl.BlockSpec(memory_space=pl.ANY)]
    nd_out = len(out_shape)
    grid_spec = pltpu.PrefetchScalarGridSpec(
        num_scalar_prefetch=1,
        grid=(bsz, nchunk),
        in_specs=in_specs,
        out_specs=pl.BlockSpec((1,) + out_shape[1:],
                               lambda b, c, pt_ref: (b,) + (0,) * (nd_out - 1)),
        scratch_shapes=[pltpu.VMEM((2, ppc) + kc.shape[1:], kc.dtype),
                        pltpu.VMEM((2, ppc) + vc.shape[1:], vc.dtype),
                        pltpu.SemaphoreType.DMA((2, 2)),
                        pltpu.VMEM((nrow, 1), F32),
                        pltpu.VMEM((nrow, 1), F32),
                        pltpu.VMEM((nrow, dv), F32)])
    return pl.pallas_call(
        functools.partial(body, nq=nq, nchunk=nchunk, ppc=ppc),
        grid_spec=grid_spec,
        out_shape=jax.ShapeDtypeStruct(out_shape, out_dtype),
        compiler_params=_cparams(("arbitrary", "arbitrary")),
    )(pt, q, knew, vnew, *extras, kc, vc)


def paged_diff(q, page_table, kc, vc, knew, vnew, lqk, subln, *, lam_init, out_dtype=BF16):
    bsz, nrow, _ = q.shape
    return _paged_call(functools.partial(_paged_diff_kernel, lam_init=lam_init), q, page_table,
                       kc, vc, knew, vnew, (lqk, subln), (bsz, nrow // 2, DVA), DVA, out_dtype)


def paged_mla(q, page_table, kc, rc, knew, vnew, wuv, *, out_dtype=BF16):
    bsz = q.shape[0]
    nq = knew.shape[1]
    return _paged_call(_paged_mla_kernel, q, page_table, kc, rc, knew, vnew, (wuv,),
                       (bsz, H_C, nq, DV_C), KR_C, out_dtype, page_mult=2)


def _cumsum_rows(x):
    n = x.shape[0]
    row = lax.broadcasted_iota(jnp.int32, x.shape, 0)
    s = 1
    while s < n:
        x = x + jnp.where(row >= s, pltpu.roll(x, s, axis=0), 0.0)
        s *= 2
    return x


def _hgrn_kernel(q_ref, f_ref, i_ref, g_ref, lb_ref, gn_ref, s0_ref, y_ref, sout_ref,
                 st_ref, *, chunk):
    c = pl.program_id(1)
    nc = pl.num_programs(1)
    sub = 8

    @pl.when(c == 0)
    def _():
        for h in range(H_B):
            st_ref[h] = s0_ref[0, h].T

    for h in range(H_B):
        hs = slice(h * DK_B, (h + 1) * DK_B)
        lb = lb_ref[:, hs]
        qr = q_ref[0, :, hs]
        q = qr * jax.nn.sigmoid(qr)
        fr = f_ref[0, :, hs]
        f = lb + (1.0 - lb) * jax.nn.sigmoid(fr)
        cum = _cumsum_rows(jnp.log(jnp.maximum(f, F_MIN)))
        k = (1.0 - lb) * jax.nn.sigmoid(-fr)
        v = i_ref[0, :, hs]
        st = st_ref[h]
        last = cum[chunk - 1:chunk]
        o_inter = lax.dot_general((q * jnp.exp(cum)).astype(BF16), st.astype(BF16),
                                  (((1,), (1,)), ((), ())), preferred_element_type=F32)
        kd = (k * jnp.exp(last - cum)).astype(BF16)
        st_ref[h] = st * jnp.exp(last) + lax.dot_general(
            v.astype(BF16), kd, (((0,), (0,)), ((), ())), preferred_element_type=F32)
        o_rows = []
        for ib in range(chunk // sub):
            hi = (ib + 1) * sub
            cj = cum[:hi]
            kj = k[:hi]
            vj = v[:hi]
            jrow = lax.broadcasted_iota(jnp.int32, (hi, 1), 0)
            for i in range(ib * sub, hi):
                t = q[i:i + 1] * kj * jnp.exp(jnp.minimum(cum[i:i + 1] - cj, 0.0))
                att = jnp.where(jrow <= i, jnp.sum(t, axis=-1, keepdims=True), 0.0)
                o_rows.append(jnp.sum(att * vj, axis=0, keepdims=True))
        o = jnp.concatenate(o_rows, axis=0) + o_inter
        o = o * lax.rsqrt(jnp.mean(o * o, axis=-1, keepdims=True) + NORM_EPS) * gn_ref[...]
        gr = g_ref[0, :, hs]
        y_ref[0, :, hs] = (o * (gr * jax.nn.sigmoid(gr))).astype(y_ref.dtype)

    @pl.when(c == nc - 1)
    def _():
        for h in range(H_B):
            sout_ref[0, h] = st_ref[h].T


def hgrn_scan(pb, lb, gnorm, s0, *, chunk, out_dtype=BF16):
    bsz, seq, _ = pb.shape
    chunk = min(chunk, seq)
    wide = HB_K

    def field(j):
        return pl.BlockSpec((1, chunk, wide), lambda b, c, j=j: (b, c, j))

    return pl.pallas_call(
        functools.partial(_hgrn_kernel, chunk=chunk),
        grid=(bsz, seq // chunk),
        in_specs=[field(0), field(1), field(2), field(3),
                  pl.BlockSpec((1, HB_K), lambda b, c: (0, 0)),
                  pl.BlockSpec((1, DV_B), lambda b, c: (0, 0)),
                  pl.BlockSpec((1, H_B, DK_B, DV_B), lambda b, c: (b, 0, 0, 0))],
        out_specs=[pl.BlockSpec((1, chunk, GROUP_W), lambda b, c: (b, c, 0)),
                   pl.BlockSpec((1, H_B, DK_B, DV_B), lambda b, c: (b, 0, 0, 0))],
        out_shape=[jax.ShapeDtypeStruct((bsz, seq, GROUP_W), out_dtype),
                   jax.ShapeDtypeStruct((bsz, H_B, DK_B, DV_B), F32)],
        scratch_shapes=[pltpu.VMEM((H_B, DV_B, DK_B), F32)],
        compiler_params=_cparams(("parallel", "arbitrary")),
    )(pb, pb, pb, pb, lb.reshape(1, HB_K).astype(F32), gnorm.reshape(1, DV_B).astype(F32), s0)


WKV_SLOTS = 64


def _wkv_kernel(r_ref, w_ref, k_ref, a_ref, b_ref, v_ref, s0_ref, y_ref, sout_ref, st_ref, z_ref,
                *, ntok):
    t0 = pl.program_id(1)
    nt = pl.num_programs(1)
    npair = H_D // 2
    nrow = r_ref.shape[1]

    @pl.when(t0 == 0)
    def _():
        st_ref[...] = s0_ref[0]

    lane = lax.broadcasted_iota(jnp.int32, (N_D, LANES), 1)
    low = lane < WKV_SLOTS
    base = (lane // WKV_SLOTS) * WKV_SLOTS
    nhalf = -(-ntok // WKV_SLOTS)

    for vi, ref in enumerate((r_ref, w_ref, k_ref, a_ref, b_ref)):
        for p in range(npair):
            blk = ref[0, :, p * LANES:(p + 1) * LANES]
            if nrow < LANES:
                blk = jnp.concatenate([blk, jnp.zeros((LANES - nrow, LANES), F32)], axis=0)
            xt = blk.T
            top, bot = xt[:N_D], xt[N_D:]
            z_ref[vi, p, 0] = jnp.where(low, top, pltpu.roll(bot, WKV_SLOTS, axis=1))
            if nhalf > 1:
                z_ref[vi, p, 1] = jnp.where(low, pltpu.roll(top, WKV_SLOTS, axis=1), bot)

    sub = 8
    states = tuple(st_ref[p] for p in range(npair))
    for half in range(nhalf):
        left = min(WKV_SLOTS, ntok - half * WKV_SLOTS)

        def body(tb, states, half=half, left=left):
            r0 = pl.multiple_of(half * WKV_SLOTS + tb * sub, sub)
            vt = v_ref[0, pl.ds(r0, sub), :]
            states = list(states)
            yrows = [[] for _ in range(npair)]
            for j in range(min(sub, left)):
                idx = base + (tb * sub + j)
                for p in range(npair):
                    st = states[p]
                    col = lambda vi: jnp.take_along_axis(z_ref[vi, p, half], idx, axis=1)
                    vrow = vt[j:j + 1, p * LANES:(p + 1) * LANES]
                    sa = jnp.sum(st * col(3), axis=0, keepdims=True)
                    st = st * col(1) + col(4) * sa + col(2) * vrow
                    yrows[p].append(jnp.sum(st * col(0), axis=0, keepdims=True))
                    states[p] = st
            pad = [jnp.zeros((sub - len(yrows[0]), LANES), F32)] if len(yrows[0]) < sub else []
            y_ref[0, pl.ds(r0, sub), :] = jnp.concatenate(
                [jnp.concatenate(yrows[p] + pad, axis=0) for p in range(npair)], axis=1)
            return tuple(states)

        states = lax.fori_loop(0, -(-left // sub), body, states)
    for p in range(npair):
        st_ref[p] = states[p]

    @pl.when(t0 == nt - 1)
    def _():
        sout_ref[0] = st_ref[...]


def wkv_scan(r, decay, k, a, b, v, s0):
    bsz, seq, _ = r.shape
    npair = H_D // 2
    ntok = min(LANES, seq)
    nblk = seq // ntok
    nrow = max(ntok, 8)
    rows = [r, decay, k, a, b, v]
    if nrow > ntok:
        rows = [jnp.pad(x, ((0, 0), (0, nrow - ntok), (0, 0))) for x in rows]
    st0 = jnp.transpose(s0.reshape(bsz, npair, 2, N_D, N_D), (0, 1, 4, 2, 3)).reshape(
        bsz, npair, N_D, LANES)
    row_spec = pl.BlockSpec((1, nrow, GROUP_W), lambda bi, ti: (bi, ti, 0))
    st_spec = pl.BlockSpec((1, npair, N_D, LANES), lambda bi, ti: (bi, 0, 0, 0))
    y, st = pl.pallas_call(
        functools.partial(_wkv_kernel, ntok=ntok),
        grid=(bsz, nblk),
        in_specs=[row_spec] * 6 + [st_spec],
        out_specs=[row_spec, st_spec],
        out_shape=[jax.ShapeDtypeStruct((bsz, nblk * nrow, GROUP_W), F32),
                   jax.ShapeDtypeStruct((bsz, npair, N_D, LANES), F32)],
        scratch_shapes=[pltpu.VMEM((npair, N_D, LANES), F32),
                        pltpu.VMEM((5, npair, 2, N_D, LANES), F32)],
        compiler_params=_cparams(("parallel", "arbitrary")),
    )(*rows, st0)
    y = y[:, :seq]
    s = jnp.transpose(st.reshape(bsz, npair, N_D, 2, N_D), (0, 1, 3, 4, 2)).reshape(
        bsz, H_D, N_D, N_D)
    return y, s


def _rope(x, pos):
    d = x.shape[-1]
    inv_freq = ROPE_THETA ** (-jnp.arange(0, d, 2, dtype=F32) / d)
    ang = pos.astype(F32)[:, None] * inv_freq[None, :]
    cos = jnp.cos(ang)[:, None, :]
    sin = jnp.sin(ang)[:, None, :]
    x1, x2 = x[..., : d // 2], x[..., d // 2:]
    return jnp.concatenate([x1 * cos - x2 * sin, x2 * cos + x1 * sin], axis=-1)


def _diff_queries(q, bsz, seq):
    q = q.reshape(bsz, seq, KVH_A, 2, 2, DA) * (DA ** -0.5)
    q = jnp.transpose(q, (0, 2, 4, 3, 1, 5))
    z = jnp.zeros_like(q[:, :, 0])
    q0 = jnp.concatenate([q[:, :, 0], z], axis=-1)
    q1 = jnp.concatenate([z, q[:, :, 1]], axis=-1)
    return jnp.concatenate([q0, q1], axis=2)


def _layer(x, n_prompt, bp, sp, bs, ss, pos_p, pos_s, layer_idx, lp, caches, states, page_table):
    rows = x.shape[0]
    h = rms_rows(x, lp["g_pre_mix"], BF16)
    w_in = lp["w_in"]
    a1 = A_COLS
    b1 = a1 + B_COLS
    c1 = b1 + C_COLS
    wa = w_in[:, :a1].astype(BF16)
    wb = w_in[:, a1:b1].astype(BF16)
    wc = jnp.pad(w_in[:, b1:c1], ((0, 0), (0, 640 - C_COLS))).astype(BF16)
    wd = jnp.pad(w_in[:, c1:], ((0, 0), (0, 1792 - D_COLS))).astype(BF16)
    pa = matmul(h, wa, tn=512)
    pb = matmul(h, wb, tn=1024)
    pc = matmul(h, wc, tn=640)[:, :C_COLS]
    pd = matmul(h, wd, tn=896)[:, :D_COLS]

    def split(t):
        return t[:n_prompt].reshape(bp, sp, -1), t[n_prompt:].reshape(bs, ss, -1)

    lam_init = 0.8 - 0.6 * math.exp(-0.3 * layer_idx)
    lqk = jnp.stack([lp["diff_lq1"], lp["diff_lk1"], lp["diff_lq2"], lp["diff_lk2"]]).astype(F32)
    subln = lp["diff_subln"].reshape(1, DVA).astype(F32)

    outs = []
    pa_p, pa_s = split(pa)
    res_a = []
    for pa_g, pos, is_prompt in ((pa_p, pos_p, True), (pa_s, pos_s, False)):
        b, l, _ = pa_g.shape
        q = _rope(pa_g[..., :A_Q].reshape(b, l, 2 * H_A, DA), pos)
        k_rows = _rope(pa_g[..., A_Q:A_Q + A_K].reshape(b, l, 2 * KVH_A, DA), pos).reshape(
            b, l, KVH_A * 2 * DA)
        v_rows = pa_g[..., A_Q + A_K:]
        qd = _diff_queries(q, b, l)
        if is_prompt:
            ya = flash_prompt("diff", qd.reshape(b * KVH_A, 4, l, 2 * DA), k_rows, v_rows,
                              (lqk, subln), tq=256, tk=512, lam_init=lam_init)
        else:
            o = paged_diff(qd.reshape(b, KVH_A * 4 * l, 2 * DA), page_table,
                           caches["diff_k"], caches["diff_v"], k_rows, v_rows,
                           lqk, subln, lam_init=lam_init)
            ya = jnp.transpose(o.reshape(b, KVH_A, 2, l, DVA), (0, 3, 1, 2, 4)).reshape(b, l, GROUP_W)
        res_a.append((ya, k_rows.reshape(b, l, KVH_A, 2 * DA), v_rows.reshape(b, l, KVH_A, DVA)))

    pb_p, pb_s = split(pb)
    yb_p, sh_p = hgrn_scan(pb_p, lp["hgrn_lb"], lp["hgrn_norm"],
                           jnp.zeros((bp, H_B, DK_B, DV_B), F32), chunk=64)
    pad_s = 8 - ss
    pb_s_pad = jnp.pad(pb_s, ((0, 0), (0, pad_s), (0, 0)))
    big = jnp.full((bs, pad_s, HB_K), 1e4, F32)
    pb_s_pad = pb_s_pad.at[:, ss:, HB_K:2 * HB_K].set(big)
    yb_s, sh_s = hgrn_scan(pb_s_pad, lp["hgrn_lb"], lp["hgrn_norm"], states["hgrn"], chunk=8)
    yb_s = yb_s[:, :ss]

    pc_p, pc_s = split(pc)
    cq = rms_rows(pc[:, :QR_C], lp["mla_q_norm"], BF16)
    w_uq = lp["mla_w_uq"].reshape(QR_C, H_C * (DN_C + DR_C)).astype(BF16)
    qfull = matmul(cq, w_uq, tn=H_C * (DN_C + DR_C)).reshape(rows, H_C, DN_C + DR_C)
    w_uk = lp["mla_w_uk"]
    wabs = jnp.zeros((H_C, DN_C, H_C, KR_C), F32)
    for hh in range(H_C):
        wabs = wabs.at[hh, :, hh, :].set(w_uk[:, hh, :].T)
    q_lat = matmul(qfull[:, :, :DN_C].reshape(rows, H_C * DN_C).astype(BF16),
                   wabs.reshape(H_C * DN_C, H_C * KR_C).astype(BF16), tn=H_C * KR_C)
    q_lat = q_lat.reshape(rows, H_C, KR_C)
    c_kv = rms_rows(pc[:, QR_C:QR_C + KR_C], lp["mla_kv_norm"], F32)
    w_uv = jnp.transpose(lp["mla_w_uv"], (1, 0, 2)).astype(BF16)
    res_c = []
    scale_c = (DN_C + DR_C) ** -0.5
    for sl, b, l, pos, is_prompt in ((slice(0, n_prompt), bp, sp, pos_p, True),
                                     (slice(n_prompt, rows), bs, ss, pos_s, False)):
        q_rope = _rope(qfull[sl, :, DN_C:].reshape(b, l, H_C, DR_C), pos)
        qc = jnp.concatenate([q_lat[sl].reshape(b, l, H_C, KR_C), q_rope], axis=-1) * scale_c
        qc = jnp.transpose(qc, (0, 2, 1, 3))
        ckv = c_kv[sl].reshape(b, l, KR_C)
        k_rope = _rope(pc[sl, QR_C + KR_C:].reshape(b, l, 1, DR_C), pos).reshape(b, l, DR_C)
        kcat = jnp.concatenate([ckv, k_rope], axis=-1)
        if is_prompt:
            yc = flash_prompt("mla", qc, kcat, ckv, (w_uv,), tq=256, tk=512)
        else:
            wuv_flat = jnp.transpose(w_uv, (1, 0, 2)).reshape(KR_C, H_C * DV_C)
            o = paged_mla(qc.reshape(b, H_C * l, KR_C + DR_C), page_table,
                          caches["mla_latent"], caches["mla_rope"], kcat, ckv, wuv_flat)
            yc = jnp.transpose(o, (0, 2, 1, 3)).reshape(b, l, GROUP_W)
        res_c.append((yc, ckv, k_rope))

    pd_p, pd_s = split(pd)
    res_d = []
    for pd_g, shift_prev, s0 in ((pd_p, jnp.zeros((bp, D_COLS), F32), jnp.zeros((bp, H_D, N_D, N_D), F32)),
                                 (pd_s, states["shift"], states["rwkv"])):
        b, l, _ = pd_g.shape
        G = GROUP_W
        prev = jnp.concatenate([shift_prev[:, None, :], pd_g[:, :-1]], axis=1)
        xm = pd_g + (prev - pd_g) * lp["rwkv_mu"]
        r, k, v = xm[..., :G], xm[..., G:2 * G], xm[..., 2 * G:3 * G]
        o0 = 3 * G
        wdl = xm[..., o0:o0 + W_LORA]
        adl = xm[..., o0 + W_LORA:o0 + W_LORA + A_LORA]
        gdl = xm[..., o0 + W_LORA + A_LORA:]
        lora_in = jnp.concatenate([jnp.tanh(wdl), adl, jax.nn.sigmoid(gdl)], axis=-1)
        lora_in = jnp.pad(lora_in.reshape(b * l, 160), ((0, 0), (0, 96)))
        wl = jnp.zeros((256, 3 * G), F32)
        wl = wl.at[0:W_LORA, 0:G].set(lp["rwkv_w2"])
        wl = wl.at[W_LORA:W_LORA + A_LORA, G:2 * G].set(lp["rwkv_a2"])
        wl = wl.at[W_LORA + A_LORA:160, 2 * G:].set(lp["rwkv_g2"])
        lo = matmul(lora_in, wl.astype(BF16), tn=3 * G).reshape(b, l, 3 * G)
        w = -jax.nn.softplus(-(lp["rwkv_w0"] + lo[..., :G])) - 0.5
        decay = jnp.exp(-jnp.exp(w))
        a = jax.nn.sigmoid(lp["rwkv_a0"] + lo[..., G:2 * G])
        g = lo[..., 2 * G:]
        kk = (k * lp["rwkv_kk"]).reshape(b, l, H_D, N_D)
        kk = kk / jnp.maximum(jnp.sqrt(jnp.sum(kk * kk, axis=-1, keepdims=True)), 1e-12)
        kk = kk.reshape(b, l, G)
        k2 = k * (1.0 + (a - 1.0) * lp["rwkv_ka"])
        y, s_new = wkv_scan(r, decay, k2, -kk, kk * a, v, s0)
        y4 = y.reshape(b, l, H_D, N_D)
        mu = jnp.mean(y4, axis=-1, keepdims=True)
        var = jnp.mean(jnp.square(y4 - mu), axis=-1, keepdims=True)
        yn = ((y4 - mu) * lax.rsqrt(var + RWKV_LN_EPS)).reshape(b, l, G)
        yn = yn * lp["rwkv_ln_w"] + lp["rwkv_ln_b"]
        bonus = jnp.sum((r * k2).reshape(b, l, H_D, N_D) * lp["rwkv_rk"], axis=-1, keepdims=True)
        bonus = (bonus * v.reshape(b, l, H_D, N_D)).reshape(b, l, G)
        yd = ((yn + bonus) * g).astype(BF16)
        res_d.append((yd, s_new, pd_g[:, -1]))

    mix_p = jnp.concatenate([res_a[0][0], yb_p, res_c[0][0], res_d[0][0]], axis=-1).reshape(n_prompt, D_MODEL)
    mix_s = jnp.concatenate([res_a[1][0], yb_s, res_c[1][0], res_d[1][0]], axis=-1).reshape(rows - n_prompt, D_MODEL)
    mix = jnp.concatenate([mix_p, mix_s], axis=0)
    x = matmul(mix, lp["w_out"].astype(BF16), tn=D_MODEL, tk=1024, post=(lp["g_post_mix"], x))
    hf = rms_rows(x, lp["g_pre_ffn"], BF16)
    u = matmul(hf, lp["w_up"].astype(BF16), tn=1024, out_dtype=BF16, act="relu2")
    x = matmul(u, lp["w_down"].astype(BF16), tn=D_MODEL, tk=1024, post=(lp["g_post_ffn"], x))
    rows_p = (res_a[0][1], res_a[0][2], res_c[0][1], res_c[0][2], sh_p, res_d[0][1], res_d[0][2])
    rows_s = (res_a[1][1], res_a[1][2], res_c[1][1], res_c[1][2], sh_s, res_d[1][1], res_d[1][2])
    return x, rows_p, rows_s


def kernel(x_prompt, x_sample, cache_diff_k, cache_diff_v, cache_mla_latent, cache_mla_rope, state_hgrn, state_rwkv, state_rwkv_shift, page_table, hgrn_lb_logits, g_pre_mix, w_in, w_out, g_post_mix, diff_lq1, diff_lk1, diff_lq2, diff_lk2, diff_subln, hgrn_norm, mla_q_norm, mla_w_uq, mla_kv_norm, mla_w_uk, mla_w_uv, rwkv_mu, rwkv_w0, rwkv_w2, rwkv_a0, rwkv_a2, rwkv_g2, rwkv_kk, rwkv_ka, rwkv_rk, rwkv_ln_w, rwkv_ln_b, g_pre_ffn, w_up, w_down, g_post_ffn):
    depth = w_in.shape[0]
    probs = jax.nn.softmax(hgrn_lb_logits.astype(F32), axis=0)
    lower_bounds = jnp.clip(jnp.cumsum(probs, axis=0) - probs[0:1], 0.0, 1.0)
    bp, sp, _ = x_prompt.shape
    bs, ss, _ = x_sample.shape
    past_len = page_table.shape[1] * PAGE_SIZE
    pos_p = jnp.arange(sp)
    pos_s = past_len + jnp.arange(ss)
    n_prompt = bp * sp
    x = jnp.concatenate([x_prompt.reshape(n_prompt, D_MODEL), x_sample.reshape(bs * ss, D_MODEL)], axis=0)
    pool = cache_diff_k.shape[1]
    caches = dict(diff_k=cache_diff_k.reshape(depth * pool, PAGE_SIZE * KVH_A, 2 * DA),
                  diff_v=cache_diff_v.reshape(depth * pool, PAGE_SIZE * KVH_A, DVA),
                  mla_latent=cache_mla_latent.reshape(depth * pool, PAGE_SIZE, KR_C),
                  mla_rope=jnp.swapaxes(cache_mla_rope, 2, 3).reshape(depth * pool, DR_C, PAGE_SIZE))
    rows_p, rows_s = [], []
    for l in range(depth):
        lp = dict(
            g_pre_mix=g_pre_mix[l], w_in=w_in[l], w_out=w_out[l], g_post_mix=g_post_mix[l],
            diff_lq1=diff_lq1[l], diff_lk1=diff_lk1[l], diff_lq2=diff_lq2[l], diff_lk2=diff_lk2[l],
            diff_subln=diff_subln[l], hgrn_lb=lower_bounds[l], hgrn_norm=hgrn_norm[l],
            mla_q_norm=mla_q_norm[l], mla_w_uq=mla_w_uq[l], mla_kv_norm=mla_kv_norm[l],
            mla_w_uk=mla_w_uk[l], mla_w_uv=mla_w_uv[l],
            rwkv_mu=rwkv_mu[l], rwkv_w0=rwkv_w0[l], rwkv_w2=rwkv_w2[l], rwkv_a0=rwkv_a0[l],
            rwkv_a2=rwkv_a2[l], rwkv_g2=rwkv_g2[l], rwkv_kk=rwkv_kk[l], rwkv_ka=rwkv_ka[l],
            rwkv_rk=rwkv_rk[l], rwkv_ln_w=rwkv_ln_w[l], rwkv_ln_b=rwkv_ln_b[l],
            g_pre_ffn=g_pre_ffn[l], w_up=w_up[l], w_down=w_down[l], g_post_ffn=g_post_ffn[l])
        states = dict(hgrn=state_hgrn[l], rwkv=state_rwkv[l], shift=state_rwkv_shift[l])
        x, rp, rs = _layer(x, n_prompt, bp, sp, bs, ss, pos_p, pos_s, l, lp, caches, states,
                           page_table + l * pool)
        rows_p.append(rp)
        rows_s.append(rs)
    xp = x[:n_prompt].reshape(bp, sp, D_MODEL)
    xs = x[n_prompt:].reshape(bs, ss, D_MODEL)
    outs_p = [jnp.stack([r[i] for r in rows_p]) for i in range(7)]
    outs_s = [jnp.stack([r[i] for r in rows_s]) for i in range(7)]
    return (xp, xs, *outs_p, *outs_s)
```
